```python
import jax, jax.numpy as jnp
from jax import lax
import numpy as np


D_MODEL = 1024
BATCH = 4
SEQ = 4096
DEPTH = 2

CHUNK = 64
N_BRANCH = 3
BRANCH_WIDTH = D_MODEL // 2
POOL_WINDOWS = (2, 4, 8, 16)
N_POOL_GROUPS = len(POOL_WINDOWS)
POOL_GROUP = BRANCH_WIDTH // N_POOL_GROUPS
CONV_K = 3
SB_HEAD_DIM = 64
SB_HEADS = BRANCH_WIDTH // SB_HEAD_DIM
Q_BLOCK = 128
RMS_EPS = 1e-6
IN_SIZES = (BRANCH_WIDTH,) * 10 + (N_BRANCH * D_MODEL,)
N_IN = sum(IN_SIZES)

kernel_name = "hybrid_pool_conv_stickbreak_block"


def _split_points():
    return [int(p) for p in np.cumsum(IN_SIZES)[:-1]]


def rms_norm(x, g):
    xf = x.astype(jnp.float32)
    y = xf * lax.rsqrt(jnp.mean(xf * xf, axis=-1, keepdims=True) + RMS_EPS)
    return (y * g.astype(jnp.float32)).astype(x.dtype)


def pool_mixer(v, w_group, scale):
    b, s, _ = v.shape
    vg = v.astype(jnp.float32).reshape(b, s, N_POOL_GROUPS, POOL_GROUP)
    csum = jnp.cumsum(vg, axis=1)
    pos = jnp.arange(s)
    outs = []
    for gi, w in enumerate(POOL_WINDOWS):
        c = csum[:, :, gi]
        lag = jnp.pad(c[:, :s - w], ((0, 0), (w, 0), (0, 0)))
        cnt = jnp.minimum(pos + 1, w).astype(jnp.float32)[None, :, None]
        outs.append((c - lag) / cnt - vg[:, :, gi])
    pooled = jnp.stack(outs, axis=2).astype(v.dtype)
    mixed = jnp.einsum('bsgc,gcd->bsgd', pooled, w_group)
    return mixed.reshape(b, s, BRANCH_WIDTH) * scale


def conv_mixer(xc, gate_b, gate_c, w, bias):
    z = gate_c * xc
    y = lax.conv_general_dilated(
        z, w[:, None, :].astype(z.dtype), window_strides=(1,), padding=[(CONV_K - 1, 0)],
        dimension_numbers=('NWC', 'WIO', 'NWC'), feature_group_count=BRANCH_WIDTH)
    return gate_b * (y + bias)


def stick_breaking_attention(q, k, v):
    b, s, _ = q.shape
    nblk = s // Q_BLOCK
    qb = q.reshape(b, nblk, Q_BLOCK, SB_HEADS, SB_HEAD_DIM).transpose(1, 0, 2, 3, 4)
    kf = k.reshape(b, s, SB_HEADS, SB_HEAD_DIM).astype(jnp.float32)
    vf = v.reshape(b, s, SB_HEADS, SB_HEAD_DIM).astype(jnp.float32)
    key_pos = jnp.arange(s)
    scale = SB_HEAD_DIM ** -0.5

    def block(args):
        qi, i = args
        logits = jnp.einsum('bqhd,bkhd->bhqk', qi.astype(jnp.float32), kf) * scale
        q_pos = i * Q_BLOCK + jnp.arange(Q_BLOCK)
        mask = key_pos[None, :] < q_pos[:, None]
        log_keep = jnp.where(mask, jax.nn.log_sigmoid(-logits), 0.0)
        later = lax.cumsum(log_keep, axis=3, reverse=True) - log_keep
        weights = jnp.where(mask, jnp.exp(jax.nn.log_sigmoid(logits) + later), 0.0)
        return jnp.einsum('bhqk,bkhd->bqhd', weights, vf)

    out = lax.map(block, (qb, jnp.arange(nblk)))
    return out.transpose(1, 0, 2, 3, 4).reshape(b, s, BRANCH_WIDTH).astype(q.dtype)


def hybrid_layer(x, g_pre, w_in, pool_w, pool_scale, conv_w, conv_b, w_branch, w_out, g_post):
    b, s, _ = x.shape
    h = rms_norm(x, g_pre)
    u = jnp.einsum('bsd,dn->bsn', h, w_in)
    (pool_v, pool_g, conv_x, conv_gb, conv_gc, conv_g,
     sb_q, sb_k, sb_v, sb_g, merge) = jnp.split(u, _split_points(), axis=-1)
    y_pool = pool_mixer(pool_v, pool_w, pool_scale) * jax.nn.silu(pool_g)
    y_conv = conv_mixer(conv_x, conv_gb, conv_gc, conv_w, conv_b) * jax.nn.silu(conv_g)
    y_sb = stick_breaking_attention(sb_q, sb_k, sb_v) * jax.nn.silu(sb_g)
    branches = jnp.stack([y_pool, y_conv, y_sb], axis=2)
    proj = jnp.einsum('bsnw,nwd->bsnd', branches, w_branch)
    gates = jax.nn.sigmoid(merge.reshape(b, s, N_BRANCH, D_MODEL))
    merged = jnp.sum(gates * proj, axis=2)
    out = jnp.einsum('bsd,de->bse', merged, w_out)
    return x + rms_norm(out, g_post)


def setup_inputs(seed: int = 0) -> dict:
    key = jax.random.key(seed)
    ks = jax.random.split(key, 10)
    f32 = jnp.float32
    x = jax.random.normal(ks[0], (BATCH, SEQ, D_MODEL), f32)
    pre_norm_g = 1.0 + 0.05 * jax.random.normal(ks[1], (DEPTH, D_MODEL), f32)
    w_in = jax.random.normal(ks[2], (DEPTH, D_MODEL, N_IN), f32) * D_MODEL ** -0.5
    pool_w = jax.random.normal(ks[3], (DEPTH, N_POOL_GROUPS, POOL_GROUP, POOL_GROUP), f32) * POOL_GROUP ** -0.5
    pool_scale = 1.0 + 0.1 * jax.random.normal(ks[4], (DEPTH, BRANCH_WIDTH), f32)
    conv_w = jax.random.normal(ks[5], (DEPTH, CONV_K, BRANCH_WIDTH), f32) * CONV_K ** -0.5
    conv_b = 0.01 * jax.random.normal(ks[6], (DEPTH, BRANCH_WIDTH), f32)
    w_branch = jax.random.normal(ks[7], (DEPTH, N_BRANCH, BRANCH_WIDTH, D_MODEL), f32) * BRANCH_WIDTH ** -0.5
    w_out = jax.random.normal(ks[8], (DEPTH, D_MODEL, D_MODEL), f32) * D_MODEL ** -0.5
    post_norm_g = 1.0 + 0.05 * jax.random.normal(ks[9], (DEPTH, D_MODEL), f32)
    return {"x": x, "pre_norm_g": pre_norm_g, "w_in": w_in, "pool_w": pool_w,
            "pool_scale": pool_scale, "conv_w": conv_w, "conv_b": conv_b,
            "w_branch": w_branch, "w_out": w_out, "post_norm_g": post_norm_g}


def reference(x, pre_norm_g, w_in, pool_w, pool_scale, conv_w, conv_b, w_branch, w_out, post_norm_g):
    for l in range(DEPTH):
        x = hybrid_layer(x, pre_norm_g[l], w_in[l], pool_w[l], pool_scale[l], conv_w[l],
                         conv_b[l], w_branch[l], w_out[l], post_norm_g[l])
    return x
```

```python
import functools

import jax
import jax.numpy as jnp
from jax import lax
from jax.experimental import pallas as pl
from jax.experimental.pallas import tpu as pltpu

F32 = jnp.float32
BF16 = jnp.bfloat16

LANES = 128
HALO = 16
POOL_WINDOWS = (2, 4, 8, 16)
POOL_GROUP = 128
CONV_K = 3
HEAD_DIM = 64
RMS_EPS = 1e-6

IN_TM = 1024
IN_TN = 1024
SB_T = 128
MIX_TM = 256
VMEM_LIMIT = 56 * 1024 * 1024


def _sigmoid(x):
    return 1.0 / (1.0 + jnp.exp(-x))


def _silu(x):
    return x * _sigmoid(x)


def _softplus(z):
    return jnp.maximum(z, 0.0) + jnp.log(1.0 + jnp.exp(-jnp.abs(z)))


def _in_proj_kernel(x_ref, g_ref, w_ref, u_ref, h_ref):
    @pl.when(pl.program_id(1) == 0)
    def _():
        x = x_ref[...]
        ms = jnp.mean(x * x, axis=-1, keepdims=True)
        h_ref[...] = (x * lax.rsqrt(ms + RMS_EPS) * g_ref[...]).astype(BF16)

    u_ref[...] = jnp.dot(h_ref[...], w_ref[...], preferred_element_type=F32).astype(u_ref.dtype)


def _in_proj(x, g, w):
    m, d = x.shape
    n = w.shape[1]
    return pl.pallas_call(
        _in_proj_kernel,
        out_shape=jax.ShapeDtypeStruct((m, n), BF16),
        grid=(m // IN_TM, n // IN_TN),
        in_specs=[
            pl.BlockSpec((IN_TM, d), lambda i, j: (i, 0)),
            pl.BlockSpec((1, d), lambda i, j: (0, 0)),
            pl.BlockSpec((d, IN_TN), lambda i, j: (0, j)),
        ],
        out_specs=pl.BlockSpec((IN_TM, IN_TN), lambda i, j: (i, j)),
        scratch_shapes=[pltpu.VMEM((IN_TM, d), BF16)],
        compiler_params=pltpu.CompilerParams(
            dimension_semantics=("parallel", "arbitrary"), vmem_limit_bytes=VMEM_LIMIT),
        name="in_proj",
    )(x, g, w)


def _stick_break_kernel(q_ref, k_ref, v_ref, g_ref, o_ref, acc_ref, run_ref):
    t = SB_T
    i = pl.program_id(2)
    q = q_ref[...]
    lane = lax.broadcasted_iota(jnp.int32, (t, LANES), 1)
    first = lane < HEAD_DIM
    zero = jnp.zeros_like(q)
    scale = jnp.asarray(HEAD_DIM ** -0.5, q.dtype)
    q2 = jnp.concatenate([jnp.where(first, q, zero), jnp.where(first, zero, q)], axis=0) * scale

    r = lax.broadcasted_iota(jnp.int32, (t, 2 * t), 0)
    c = lax.broadcasted_iota(jnp.int32, (t, 2 * t), 1)
    suffix = jnp.where((r > c) | (c >= t), 1.0, 0.0).astype(BF16)

    def tile(j, mask):
        kt = k_ref[pl.ds(j * t, t), :]
        vt = v_ref[pl.ds(j * t, t), :]
        z = lax.dot_general(q2, kt, (((1,), (1,)), ((), ())), preferred_element_type=F32)
        sp = _softplus(z)
        log_beta = z - sp
        if mask is not None:
            sp = jnp.where(mask, sp, 0.0)
        cum = jnp.dot(sp.astype(BF16), suffix, preferred_element_type=F32)
        w = jnp.exp(log_beta - cum[:, :t] - run_ref[...])
        if mask is not None:
            w = jnp.where(mask, w, 0.0)
        acc_ref[...] += jnp.dot(w.astype(BF16), vt, preferred_element_type=F32)
        run_ref[...] += cum[:, t:]

    acc_ref[...] = jnp.zeros_like(acc_ref)
    run_ref[...] = jnp.zeros_like(run_ref)
    row = lax.broadcasted_iota(jnp.int32, (2 * t, t), 0)
    col = lax.broadcasted_iota(jnp.int32, (2 * t, t), 1)
    tile(i, col < jnp.where(row >= t, row - t, row))

    def older(jj, carry):
        tile(i - 1 - jj, None)
        return carry

    lax.fori_loop(0, i, older, 0)

    out = jnp.where(first, acc_ref[:t, :], acc_ref[t:, :])
    o_ref[...] = (out * _silu(g_ref[...].astype(F32))).astype(o_ref.dtype)


def _stick_break(u, width, q_col, k_col, v_col, g_col):
    b, s, _ = u.shape
    t = SB_T
    pairs = width // LANES
    qb, kb, vb, gb = (c // LANES for c in (q_col, k_col, v_col, g_col))
    return pl.pallas_call(
        _stick_break_kernel,
        out_shape=jax.ShapeDtypeStruct((b, s, width), BF16),
        grid=(b, pairs, s // t),
        in_specs=[
            pl.BlockSpec((None, t, LANES), lambda bi, p, i: (bi, i, qb + p)),
            pl.BlockSpec((None, s, LANES), lambda bi, p, i: (bi, 0, kb + p)),
            pl.BlockSpec((None, s, LANES), lambda bi, p, i: (bi, 0, vb + p)),
            pl.BlockSpec((None, t, LANES), lambda bi, p, i: (bi, i, gb + p)),
        ],
        out_specs=pl.BlockSpec((None, t, LANES), lambda bi, p, i: (bi, i, p)),
        scratch_shapes=[pltpu.VMEM((2 * t, LANES), F32), pltpu.VMEM((2 * t, LANES), F32)],
        compiler_params=pltpu.CompilerParams(
            dimension_semantics=("parallel", "parallel", "arbitrary")),
        name="stick_break",
    )(u, u, u, u)


def _mix_out_kernel(seq_tiles, ua_ref, halo_ref, ysb_ref, m0_ref, m1_ref, m2_ref, x_ref,
                    pw_ref, ps_ref, cw_ref, cb_ref, wb_ref, wo_ref, gp_ref, o_ref):
    tm = x_ref.shape[0]
    w = ysb_ref.shape[1]
    ti = pl.program_id(0) % seq_tiles

    def ext(col):
        head = jnp.where(ti == 0, 0.0, halo_ref[:, col:col + w].astype(F32))
        return jnp.concatenate([head, ua_ref[:, col:col + w].astype(F32)], axis=0)

    def back(a, r):
        return pltpu.roll(a, r, axis=0)

    pos = ti * tm + lax.broadcasted_iota(jnp.int32, (tm, 1), 0)
    v = ext(0)
    mixed = []
    for gi, win in enumerate(POOL_WINDOWS):
        vg = v[:, gi * POOL_GROUP:(gi + 1) * POOL_GROUP]
        ssum = vg
        span = 1
        while span < win:
            ssum = ssum + back(ssum, span)
            span *= 2
        inv_cnt = 1.0 / jnp.minimum(pos + 1, win).astype(F32)
        pooled = ssum[HALO:] * inv_cnt - vg[HALO:]
        mixed.append(jnp.dot(pooled.astype(BF16), pw_ref[gi], preferred_element_type=F32))
    y_pool = jnp.concatenate(mixed, axis=1) * ps_ref[...] * _silu(ua_ref[:, w:2 * w].astype(F32))

    z = ext(4 * w) * ext(2 * w)
    y = cw_ref[0:1, :] * back(z, 2)[HALO:] + cw_ref[1:2, :] * back(z, 1)[HALO:] + cw_ref[2:3, :] * z[HALO:]
    y_conv = (ua_ref[:, 3 * w:4 * w].astype(F32) * (y + cb_ref[...])
              * _silu(ua_ref[:, 5 * w:6 * w].astype(F32)))

    merged = None
    for n, (yb, m_ref) in enumerate(((y_pool.astype(BF16), m0_ref), (y_conv.astype(BF16), m1_ref),
                                     (ysb_ref[...], m2_ref))):
        proj = jnp.dot(yb, wb_ref[n], preferred_element_type=F32)
        gated = _sigmoid(m_ref[...].astype(F32)) * proj
        merged = gated if merged is None else merged + gated

    out = jnp.dot(merged.astype(BF16), wo_ref[...], preferred_element_type=F32)
    ms = jnp.mean(out * out, axis=-1, keepdims=True)
    o_ref[...] = x_ref[...] + out * lax.rsqrt(ms + RMS_EPS) * gp_ref[...]


def _mix_out(u, ysb, x, seq_len, pool_w, pool_scale, conv_w, conv_b, w_branch, w_out, g_post):
    m, d = x.shape
    w = ysb.shape[1]
    tm = MIX_TM
    halo_blocks = tm // HALO
    merge_blk = (10 * w) // d
    const2 = lambda i: (0, 0)
    const3 = lambda i: (0, 0, 0)
    return pl.pallas_call(
        functools.partial(_mix_out_kernel, seq_len // tm),
        out_shape=jax.ShapeDtypeStruct((m, d), F32),
        grid=(m // tm,),
        in_specs=[
            pl.BlockSpec((tm, 6 * w), lambda i: (i, 0)),
            pl.BlockSpec((HALO, 6 * w), lambda i: (jnp.maximum(i * halo_blocks - 1, 0), 0)),
            pl.BlockSpec((tm, w), lambda i: (i, 0)),
            pl.BlockSpec((tm, d), lambda i: (i, merge_blk)),
            pl.BlockSpec((tm, d), lambda i: (i, merge_blk + 1)),
            pl.BlockSpec((tm, d), lambda i: (i, merge_blk + 2)),
            pl.BlockSpec((tm, d), lambda i: (i, 0)),
            pl.BlockSpec(pool_w.shape, const3),
            pl.BlockSpec(pool_scale.shape, const2),
            pl.BlockSpec(conv_w.shape, const2),
            pl.BlockSpec(conv_b.shape, const2),
            pl.BlockSpec(w_branch.shape, const3),
            pl.BlockSpec(w_out.shape, const2),
            pl.BlockSpec(g_post.shape, const2),
        ],
        out_specs=pl.BlockSpec((tm, d), lambda i: (i, 0)),
        compiler_params=pltpu.CompilerParams(
            dimension_semantics=("parallel",), vmem_limit_bytes=VMEM_LIMIT),
        name="mix_out",
    )(u, u, ysb, u, u, u, x, pool_w, pool_scale, conv_w, conv_b, w_branch, w_out, g_post)


def kernel(x, pre_norm_g, w_in, pool_w, pool_scale, conv_w, conv_b, w_branch, w_out, post_norm_g):
    b, s, d = x.shape
    depth = w_in.shape[0]
    n_in = w_in.shape[2]
    w = pool_scale.shape[1]
    assert n_in == 10 * w + 3 * d and w % LANES == 0 and (10 * w) % d == 0
    assert (b * s) % IN_TM == 0 and n_in % IN_TN == 0 and s % SB_T == 0 and s % MIX_TM == 0
    xf = x.reshape(b * s, d)
    for l in range(depth):
        u = _in_proj(xf, pre_norm_g[l].reshape(1, d), w_in[l].astype(BF16))
        ysb = _stick_break(u.reshape(b, s, n_in), w, 6 * w, 7 * w, 8 * w, 9 * w)
        xf = _mix_out(u, ysb.reshape(b * s, w), xf, s,
                      pool_w[l].astype(BF16), pool_scale[l].reshape(1, w),
                      conv_w[l], conv_b[l].reshape(1, w),
                      w_branch[l].astype(BF16), w_out[l].astype(BF16),
                      post_norm_g[l].reshape(1, d))
    return xf.reshape(b, s, d)
```

```python
import functools

import jax
import jax.numpy as jnp
from jax import lax
from jax.experimental import pallas as pl
from jax.experimental.pallas import tpu as pltpu

F32 = jnp.float32
BF16 = jnp.bfloat16

LANES = 128
HALO = 16
POOL_WINDOWS = (2, 4, 8, 16)
POOL_GROUP = 128
CONV_K = 3
HEAD_DIM = 64
RMS_EPS = 1e-6

IN_TM = 1024
IN_TN = 1024
SB_TQ = 128
SB_SUB = 64
SB_WIN = 256
MIX_TM = 256
VMEM_LIMIT = 56 * 1024 * 1024

SB_DEAD_MASS = 106.0


def _sigmoid(x):
    return 1.0 / (1.0 + jnp.exp(-x))


def _silu(x):
    return x * _sigmoid(x)


def _softplus(z):
    return jnp.maximum(z, 0.0) + jnp.log(1.0 + jnp.exp(-jnp.abs(z)))


def _in_proj_kernel(x_ref, g_ref, w_ref, u_ref, h_ref):
    @pl.when(pl.program_id(1) == 0)
    def _():
        x = x_ref[...]
        ms = jnp.mean(x * x, axis=-1, keepdims=True)
        h_ref[...] = (x * lax.rsqrt(ms + RMS_EPS) * g_ref[...]).astype(BF16)

    u_ref[...] = jnp.dot(h_ref[...], w_ref[...], preferred_element_type=F32).astype(u_ref.dtype)


def _in_proj(x, g, w):
    m, d = x.shape
    n = w.shape[1]
    return pl.pallas_call(
        _in_proj_kernel,
        out_shape=jax.ShapeDtypeStruct((m, n), BF16),
        grid=(m // IN_TM, n // IN_TN),
        in_specs=[
            pl.BlockSpec((IN_TM, d), lambda i, j: (i, 0)),
            pl.BlockSpec((1, d), lambda i, j: (0, 0)),
            pl.BlockSpec((d, IN_TN), lambda i, j: (0, j)),
        ],
        out_specs=pl.BlockSpec((IN_TM, IN_TN), lambda i, j: (i, j)),
        scratch_shapes=[pltpu.VMEM((IN_TM, d), BF16)],
        compiler_params=pltpu.CompilerParams(
            dimension_semantics=("parallel", "arbitrary"), vmem_limit_bytes=VMEM_LIMIT),
        name="in_proj",
    )(x, g, w)


def _stick_break_kernel(q_ref, k_ref, v_ref, g_ref, suf_ref, o_ref, acc_ref, run_ref):
    sub, win = SB_SUB, SB_WIN
    n_sub = SB_TQ // sub
    n_pair = q_ref.shape[1] // LANES
    hist = win - sub
    q0 = pl.program_id(1) * SB_TQ

    lane = lax.broadcasted_iota(jnp.int32, (sub, LANES), 1)
    first = lane < HEAD_DIM
    col = lax.broadcasted_iota(jnp.int32, (2 * sub, win), 1)
    rsub = lax.broadcasted_iota(jnp.int32, (2 * sub, win), 0) & (sub - 1)
    scale = jnp.asarray(HEAD_DIM ** -0.5, BF16)
    nt = (((1,), (1,)), ((), ()))

    def stacked_q(s, p):
        q = q_ref[s * sub:(s + 1) * sub, p * LANES:(p + 1) * LANES]
        zero = jnp.zeros_like(q)
        return jnp.concatenate([jnp.where(first, q, zero), jnp.where(first, zero, q)], axis=0) * scale

    def key_tile(s, p, start, mask, run):
        kt = k_ref[pl.ds(start, win), p * LANES:(p + 1) * LANES]
        vt = v_ref[pl.ds(start, win), p * LANES:(p + 1) * LANES]
        z = lax.dot_general(stacked_q(s, p), kt, nt, preferred_element_type=F32)
        sp = _softplus(z)
        log_beta = z - sp
        sp = jnp.where(mask, sp, 0.0)
        cum = jnp.dot(sp.astype(BF16), suf_ref[...], preferred_element_type=F32)
        later = cum if run is None else cum + run
        w = jnp.where(mask, jnp.exp(log_beta - later), 0.0)
        pv = jnp.dot(w.astype(BF16), vt, preferred_element_type=F32)
        return pv, cum[:, 0:1] + sp[:, 0:1]

    def emit(s, p, acc):
        rows = slice(s * sub, (s + 1) * sub)
        cols = slice(p * LANES, (p + 1) * LANES)
        out = jnp.where(first, acc[:sub, :], acc[sub:, :])
        o_ref[rows, cols] = (out * _silu(g_ref[rows, cols].astype(F32))).astype(o_ref.dtype)

    win_start = [pl.multiple_of(jnp.maximum(q0 + s * sub - hist, 0), sub) for s in range(n_sub)]
    accs, masses = {}, {}
    for s in range(n_sub):
        mask = col < rsub + (q0 + s * sub - win_start[s])
        for p in range(n_pair):
            accs[s, p], masses[s, p] = key_tile(s, p, win_start[s], mask, None)
            emit(s, p, accs[s, p])

    least = functools.reduce(jnp.minimum, masses.values())
    older_end = win_start[n_sub - 1]

    @pl.when((older_end > 0) & (jnp.min(least) < SB_DEAD_MASS))
    def _():
        for idx, key in enumerate(accs):
            acc_ref[idx] = accs[key]
            run_ref[idx] = jnp.broadcast_to(masses[key], (2 * sub, LANES))

        def live(carry):
            end, least_mass = carry
            return (end > 0) & (least_mass < SB_DEAD_MASS)

        def older(carry):
            end, _ = carry
            start = pl.multiple_of(jnp.maximum(end - win, 0), sub)
            least_mass = None
            for idx, (s, p) in enumerate(accs):
                mask = col + start < jnp.minimum(win_start[s], end)
                run = run_ref[idx]
                pv, mass = key_tile(s, p, start, mask, jnp.concatenate([run] * (win // LANES), axis=1))
                acc_ref[idx] += pv
                run = run + jnp.broadcast_to(mass, run.shape)
                run_ref[idx] = run
                low = jnp.min(run)
                least_mass = low if least_mass is None else jnp.minimum(least_mass, low)
            return start, least_mass

        lax.while_loop(live, older, (older_end, jnp.min(least)))
        for idx, (s, p) in enumerate(accs):
            emit(s, p, acc_ref[idx])


def _stick_break(u, suffix, width, q_col, k_col, v_col, g_col):
    b, s, _ = u.shape
    n_chain = (SB_TQ // SB_SUB) * (width // LANES)
    qb, kb, vb, gb = (c // width for c in (q_col, k_col, v_col, g_col))
    return pl.pallas_call(
        _stick_break_kernel,
        out_shape=jax.ShapeDtypeStruct((b, s, width), BF16),
        grid=(b, s // SB_TQ),
        in_specs=[
            pl.BlockSpec((None, SB_TQ, width), lambda bi, i: (bi, i, qb)),
            pl.BlockSpec((None, s, width), lambda bi, i: (bi, 0, kb)),
            pl.BlockSpec((None, s, width), lambda bi, i: (bi, 0, vb)),
            pl.BlockSpec((None, SB_TQ, width), lambda bi, i: (bi, i, gb)),
            pl.BlockSpec(suffix.shape, lambda bi, i: (0, 0)),
        ],
        out_specs=pl.BlockSpec((None, SB_TQ, width), lambda bi, i: (bi, i, 0)),
        scratch_shapes=[pltpu.VMEM((n_chain, 2 * SB_SUB, LANES), F32),
                        pltpu.VMEM((n_chain, 2 * SB_SUB, LANES), F32)],
        compiler_params=pltpu.CompilerParams(
            dimension_semantics=("parallel", "arbitrary"), vmem_limit_bytes=VMEM_LIMIT),
        name="stick_break",
    )(u, u, u, u, suffix)


def _mix_out_kernel(seq_tiles, ua_ref, halo_ref, ysb_ref, m0_ref, m1_ref, m2_ref, x_ref,
                    pw_ref, ps_ref, cw_ref, cb_ref, wb_ref, wo_ref, gp_ref, o_ref):
    tm = x_ref.shape[0]
    w = ysb_ref.shape[1]
    ti = pl.program_id(0) % seq_tiles

    def ext(col):
        head = jnp.where(ti == 0, 0.0, halo_ref[:, col:col + w].astype(F32))
        return jnp.concatenate([head, ua_ref[:, col:col + w].astype(F32)], axis=0)

    def back(a, r):
        return pltpu.roll(a, r, axis=0)

    pos = ti * tm + lax.broadcasted_iota(jnp.int32, (tm, 1), 0)
    v = ext(0)
    mixed = []
    for gi, win in enumerate(POOL_WINDOWS):
        vg = v[:, gi * POOL_GROUP:(gi + 1) * POOL_GROUP]
        ssum = vg
        span = 1
        while span < win:
            ssum = ssum + back(ssum, span)
            span *= 2
        inv_cnt = 1.0 / jnp.minimum(pos + 1, win).astype(F32)
        pooled = ssum[HALO:] * inv_cnt - vg[HALO:]
        mixed.append(jnp.dot(pooled.astype(BF16), pw_ref[gi], preferred_element_type=F32))
    y_pool = jnp.concatenate(mixed, axis=1) * ps_ref[...] * _silu(ua_ref[:, w:2 * w].astype(F32))

    z = ext(4 * w) * ext(2 * w)
    y = cw_ref[0:1, :] * back(z, 2)[HALO:] + cw_ref[1:2, :] * back(z, 1)[HALO:] + cw_ref[2:3, :] * z[HALO:]
    y_conv = (ua_ref[:, 3 * w:4 * w].astype(F32) * (y + cb_ref[...])
              * _silu(ua_ref[:, 5 * w:6 * w].astype(F32)))

    merged = None
    for n, (yb, m_ref) in enumerate(((y_pool.astype(BF16), m0_ref), (y_conv.astype(BF16), m1_ref),
                                     (ysb_ref[...], m2_ref))):
        proj = jnp.dot(yb, wb_ref[n], preferred_element_type=F32)
        gated = _sigmoid(m_ref[...].astype(F32)) * proj
        merged = gated if merged is None else merged + gated

    out = jnp.dot(merged.astype(BF16), wo_ref[...], preferred_element_type=F32)
    ms = jnp.mean(out * out, axis=-1, keepdims=True)
    o_ref[...] = x_ref[...] + out * lax.rsqrt(ms + RMS_EPS) * gp_ref[...]


def _mix_out(u, ysb, x, seq_len, pool_w, pool_scale, conv_w, conv_b, w_branch, w_out, g_post):
    m, d = x.shape
    w = ysb.shape[1]
    tm = MIX_TM
    halo_blocks = tm // HALO
    merge_blk = (10 * w) // d
    const2 = lambda i: (0, 0)
    const3 = lambda i: (0, 0, 0)
    return pl.pallas_call(
        functools.partial(_mix_out_kernel, seq_len // tm),
        out_shape=jax.ShapeDtypeStruct((m, d), F32),
        grid=(m // tm,),
        in_specs=[
            pl.BlockSpec((tm, 6 * w), lambda i: (i, 0)),
            pl.BlockSpec((HALO, 6 * w), lambda i: (jnp.maximum(i * halo_blocks - 1, 0), 0)),
            pl.BlockSpec((tm, w), lambda i: (i, 0)),
            pl.BlockSpec((tm, d), lambda i: (i, merge_blk)),
            pl.BlockSpec((tm, d), lambda i: (i, merge_blk + 1)),
            pl.BlockSpec((tm, d), lambda i: (i, merge_blk + 2)),
            pl.BlockSpec((tm, d), lambda i: (i, 0)),
            pl.BlockSpec(pool_w.shape, const3),
            pl.BlockSpec(pool_scale.shape, const2),
            pl.BlockSpec(conv_w.shape, const2),
            pl.BlockSpec(conv_b.shape, const2),
            pl.BlockSpec(w_branch.shape, const3),
            pl.BlockSpec(w_out.shape, const2),
            pl.BlockSpec(g_post.shape, const2),
        ],
        out_specs=pl.BlockSpec((tm, d), lambda i: (i, 0)),
        compiler_params=pltpu.CompilerParams(
            dimension_semantics=("parallel",), vmem_limit_bytes=VMEM_LIMIT),
        name="mix_out",
    )(u, u, ysb, u, u, u, x, pool_w, pool_scale, conv_w, conv_b, w_branch, w_out, g_post)


def kernel(x, pre_norm_g, w_in, pool_w, pool_scale, conv_w, conv_b, w_branch, w_out, post_norm_g):
    b, s, d = x.shape
    depth = w_in.shape[0]
    n_in = w_in.shape[2]
    w = pool_scale.shape[1]
    assert n_in == 10 * w + 3 * d and w % LANES == 0 and (10 * w) % d == 0
    assert (b * s) % IN_TM == 0 and n_in % IN_TN == 0 and s % SB_TQ == 0 and s % MIX_TM == 0
    assert s >= SB_WIN and SB_WIN % SB_SUB == 0
    key = jnp.arange(SB_WIN)
    suffix = (key[:, None] > key[None, :]).astype(BF16)
    xf = x.reshape(b * s, d)
    for l in range(depth):
        u = _in_proj(xf, pre_norm_g[l].reshape(1, d), w_in[l].astype(BF16))
        ysb = _stick_break(u.reshape(b, s, n_in), suffix, w, 6 * w, 7 * w, 8 * w, 9 * w)
        xf = _mix_out(u, ysb.reshape(b * s, w), xf, s,
                      pool_w[l].astype(BF16), pool_scale[l].reshape(1, w),
                      conv_w[l], conv_b[l].reshape(1, w),
                      w_branch[l].astype(BF16), w_out[l].astype(BF16),
                      post_norm_g[l].reshape(1, d))
    return xf.reshape(b, s, d)
```

```python
import functools

import jax
import jax.numpy as jnp
from jax import lax
from jax.experimental import pallas as pl
from jax.experimental.pallas import tpu as pltpu

F32 = jnp.float32
BF16 = jnp.bfloat16

LANES = 128
HALO = 16
POOL_WINDOWS = (2, 4, 8, 16)
POOL_GROUP = 128
CONV_K = 3
HEAD_DIM = 64
RMS_EPS = 1e-6

IN_TM = 2048
IN_TN = 1024
SB_TQ = 128
SB_SUB = 64
SB_WIN = 256
MIX_TM = 512
VMEM_LIMIT = 56 * 1024 * 1024

SB_DEAD_MASS = 106.0


def _sigmoid(x):
    return 1.0 / (1.0 + jnp.exp(-x))


def _silu(x):
    return x * _sigmoid(x)


def _softplus(z):
    return jnp.maximum(z, jnp.log(1.0 + jnp.exp(jnp.minimum(z, 44.0))))


def _in_proj_kernel(x_ref, g_ref, w_ref, u_ref, h_ref):
    @pl.when(pl.program_id(1) == 0)
    def _():
        x = x_ref[...]
        ms = jnp.mean(x * x, axis=-1, keepdims=True)
        h_ref[...] = (x * lax.rsqrt(ms + RMS_EPS) * g_ref[...]).astype(BF16)

    u_ref[...] = jnp.dot(h_ref[...], w_ref[...].astype(BF16),
                         preferred_element_type=F32).astype(u_ref.dtype)


def _in_proj(layer, x, g, w):
    m, d = x.shape
    n = w.shape[2]
    return pl.pallas_call(
        _in_proj_kernel,
        out_shape=jax.ShapeDtypeStruct((m, n), BF16),
        grid=(m // IN_TM, n // IN_TN),
        in_specs=[
            pl.BlockSpec((IN_TM, d), lambda i, j: (i, 0)),
            pl.BlockSpec((None, 1, d), lambda i, j: (layer, 0, 0)),
            pl.BlockSpec((None, d, IN_TN), lambda i, j: (layer, 0, j)),
        ],
        out_specs=pl.BlockSpec((IN_TM, IN_TN), lambda i, j: (i, j)),
        scratch_shapes=[pltpu.VMEM((IN_TM, d), BF16)],
        compiler_params=pltpu.CompilerParams(
            dimension_semantics=("parallel", "arbitrary"), vmem_limit_bytes=VMEM_LIMIT),
        name="in_proj",
    )(x, g, w)


def _stick_break_kernel(q_ref, k_ref, v_ref, g_ref, suf_ref, o_ref, acc_ref, run_ref):
    sub, win = SB_SUB, SB_WIN
    n_sub = SB_TQ // sub
    n_pair = q_ref.shape[1] // LANES
    hist = win - sub
    q0 = pl.program_id(1) * SB_TQ

    lane = lax.broadcasted_iota(jnp.int32, (sub, LANES), 1)
    first = lane < HEAD_DIM
    col = lax.broadcasted_iota(jnp.int32, (2 * sub, win), 1)
    rsub = lax.broadcasted_iota(jnp.int32, (2 * sub, win), 0) & (sub - 1)
    scale = jnp.asarray(HEAD_DIM ** -0.5, BF16)
    nt = (((1,), (1,)), ((), ()))

    def stacked_q(s, p):
        q = q_ref[s * sub:(s + 1) * sub, p * LANES:(p + 1) * LANES]
        zero = jnp.zeros_like(q)
        return jnp.concatenate([jnp.where(first, q, zero), jnp.where(first, zero, q)], axis=0) * scale

    def logits(s, p, start):
        kt = k_ref[pl.ds(start, win), p * LANES:(p + 1) * LANES]
        return lax.dot_general(stacked_q(s, p), kt, nt, preferred_element_type=F32)

    def weigh(s, p, start, mask, log_beta, later):
        vt = v_ref[pl.ds(start, win), p * LANES:(p + 1) * LANES]
        w = jnp.where(mask, jnp.exp(log_beta - later), 0.0)
        return jnp.dot(w.astype(BF16), vt, preferred_element_type=F32)

    def key_tile(s, p, start, mask, run):
        z = logits(s, p, start)
        sp = _softplus(z)
        log_beta = z - sp
        sp = jnp.where(mask, sp, 0.0)
        cum = jnp.dot(sp.astype(BF16), suf_ref[...], preferred_element_type=F32)
        return weigh(s, p, start, mask, log_beta, cum + run), cum[:, 0:1] + sp[:, 0:1]

    def emit(s, p, acc):
        rows = slice(s * sub, (s + 1) * sub)
        cols = slice(p * LANES, (p + 1) * LANES)
        out = jnp.where(first, acc[:sub, :], acc[sub:, :])
        o_ref[rows, cols] = (out * _silu(g_ref[rows, cols].astype(F32))).astype(o_ref.dtype)

    win_start = [pl.multiple_of(jnp.maximum(q0 + s * sub - hist, 0), sub) for s in range(n_sub)]
    masks = [col < rsub + (q0 + s * sub - win_start[s]) for s in range(n_sub)]
    chains = [(s, p) for s in range(n_sub) for p in range(n_pair)]
    rows = 2 * sub
    zs = [logits(s, p, win_start[s]) for s, p in chains]
    sps, log_betas = [], []
    for (s, p), z in zip(chains, zs):
        sp = _softplus(z)
        log_betas.append(z - sp)
        sps.append(jnp.where(masks[s], sp, 0.0))
    cum_all = jnp.dot(jnp.concatenate([sp.astype(BF16) for sp in sps], axis=0), suf_ref[...],
                      preferred_element_type=F32)
    accs, masses = {}, {}
    for idx, (s, p) in enumerate(chains):
        cum = cum_all[idx * rows:(idx + 1) * rows]
        accs[s, p] = weigh(s, p, win_start[s], masks[s], log_betas[idx], cum)
        masses[s, p] = cum[:, 0:1] + sps[idx][:, 0:1]
        emit(s, p, accs[s, p])

    least = functools.reduce(jnp.minimum, masses.values())
    older_end = win_start[n_sub - 1]

    @pl.when((older_end > 0) & (jnp.min(least) < SB_DEAD_MASS))
    def _():
        for idx, key in enumerate(accs):
            acc_ref[idx] = accs[key]
            run_ref[idx] = jnp.broadcast_to(masses[key], (2 * sub, LANES))

        def live(carry):
            end, least_mass = carry
            return (end > 0) & (least_mass < SB_DEAD_MASS)

        def older(carry):
            end, _ = carry
            start = pl.multiple_of(jnp.maximum(end - win, 0), sub)
            least_mass = None
            for idx, (s, p) in enumerate(accs):
                mask = col + start < jnp.minimum(win_start[s], end)
                run = run_ref[idx]
                pv, mass = key_tile(s, p, start, mask, jnp.concatenate([run] * (win // LANES), axis=1))
                acc_ref[idx] += pv
                run = run + jnp.broadcast_to(mass, run.shape)
                run_ref[idx] = run
                low = jnp.min(run)
                least_mass = low if least_mass is None else jnp.minimum(least_mass, low)
            return start, least_mass

        lax.while_loop(live, older, (older_end, jnp.min(least)))
        for idx, (s, p) in enumerate(accs):
            emit(s, p, acc_ref[idx])


def _stick_break(u, suffix, width, q_col, k_col, v_col, g_col):
    b, s, _ = u.shape
    n_chain = (SB_TQ // SB_SUB) * (width // LANES)
    qb, kb, vb, gb = (c // width for c in (q_col, k_col, v_col, g_col))
    return pl.pallas_call(
        _stick_break_kernel,
        out_shape=jax.ShapeDtypeStruct((b, s, width), BF16),
        grid=(b, s // SB_TQ),
        in_specs=[
            pl.BlockSpec((None, SB_TQ, width), lambda bi, i: (bi, i, qb)),
            pl.BlockSpec((None, s, width), lambda bi, i: (bi, 0, kb)),
            pl.BlockSpec((None, s, width), lambda bi, i: (bi, 0, vb)),
            pl.BlockSpec((None, SB_TQ, width), lambda bi, i: (bi, i, gb)),
            pl.BlockSpec(suffix.shape, lambda bi, i: (0, 0)),
        ],
        out_specs=pl.BlockSpec((None, SB_TQ, width), lambda bi, i: (bi, i, 0)),
        scratch_shapes=[pltpu.VMEM((n_chain, 2 * SB_SUB, LANES), F32),
                        pltpu.VMEM((n_chain, 2 * SB_SUB, LANES), F32)],
        compiler_params=pltpu.CompilerParams(
            dimension_semantics=("parallel", "arbitrary"), vmem_limit_bytes=VMEM_LIMIT),
        name="stick_break",
    )(u, u, u, u, suffix)


def _mix_out_kernel(seq_tiles, ua_ref, halo_ref, ysb_ref, m0_ref, m1_ref, m2_ref, x_ref,
                    pw_ref, ps_ref, cw_ref, cb_ref, wb_ref, wo_ref, gp_ref, o_ref):
    tm = x_ref.shape[0]
    w = ysb_ref.shape[1]
    ti = pl.program_id(0) % seq_tiles

    def ext(col):
        head = jnp.where(ti == 0, 0.0, halo_ref[:, col:col + w].astype(F32))
        return jnp.concatenate([head, ua_ref[:, col:col + w].astype(F32)], axis=0)

    def back(a, r):
        return pltpu.roll(a, r, axis=0)

    pos = ti * tm + lax.broadcasted_iota(jnp.int32, (tm, 1), 0)
    v = ext(0)
    mixed = []
    for gi, win in enumerate(POOL_WINDOWS):
        vg = v[:, gi * POOL_GROUP:(gi + 1) * POOL_GROUP]
        ssum = vg
        span = 1
        while span < win:
            ssum = ssum + back(ssum, span)
            span *= 2
        inv_cnt = 1.0 / jnp.minimum(pos + 1, win).astype(F32)
        pooled = ssum[HALO:] * inv_cnt - vg[HALO:]
        mixed.append(jnp.dot(pooled.astype(BF16), pw_ref[gi], preferred_element_type=F32))
    y_pool = jnp.concatenate(mixed, axis=1) * ps_ref[...] * _silu(ua_ref[:, w:2 * w].astype(F32))

    z = ext(4 * w) * ext(2 * w)
    y = cw_ref[0:1, :] * back(z, 2)[HALO:] + cw_ref[1:2, :] * back(z, 1)[HALO:] + cw_ref[2:3, :] * z[HALO:]
    y_conv = (ua_ref[:, 3 * w:4 * w].astype(F32) * (y + cb_ref[...])
              * _silu(ua_ref[:, 5 * w:6 * w].astype(F32)))

    merged = None
    for n, (yb, m_ref) in enumerate(((y_pool.astype(BF16), m0_ref), (y_conv.astype(BF16), m1_ref),
                                     (ysb_ref[...], m2_ref))):
        proj = jnp.dot(yb, wb_ref[n], preferred_element_type=F32)
        gated = _sigmoid(m_ref[...].astype(F32)) * proj
        merged = gated if merged is None else merged + gated

    out = jnp.dot(merged.astype(BF16), wo_ref[...], preferred_element_type=F32)
    ms = jnp.mean(out * out, axis=-1, keepdims=True)
    o_ref[...] = x_ref[...] + out * lax.rsqrt(ms + RMS_EPS) * gp_ref[...]


def _mix_out(layer, u, ysb, x, seq_len, pool_w, pool_scale, conv_w, conv_b, w_branch, w_out, g_post):
    m, d = x.shape
    w = ysb.shape[1]
    tm = MIX_TM
    halo_blocks = tm // HALO
    merge_blk = (10 * w) // d

    def param(a):
        return pl.BlockSpec((None,) + a.shape[1:], lambda i: (layer,) + (0,) * (a.ndim - 1))

    return pl.pallas_call(
        functools.partial(_mix_out_kernel, seq_len // tm),
        out_shape=jax.ShapeDtypeStruct((m, d), F32),
        grid=(m // tm,),
        in_specs=[
            pl.BlockSpec((tm, 6 * w), lambda i: (i, 0)),
            pl.BlockSpec((HALO, 6 * w), lambda i: (jnp.maximum(i * halo_blocks - 1, 0), 0)),
            pl.BlockSpec((tm, w), lambda i: (i, 0)),
            pl.BlockSpec((tm, d), lambda i: (i, merge_blk)),
            pl.BlockSpec((tm, d), lambda i: (i, merge_blk + 1)),
            pl.BlockSpec((tm, d), lambda i: (i, merge_blk + 2)),
            pl.BlockSpec((tm, d), lambda i: (i, 0)),
            param(pool_w), param(pool_scale), param(conv_w), param(conv_b),
            param(w_branch), param(w_out), param(g_post),
        ],
        out_specs=pl.BlockSpec((tm, d), lambda i: (i, 0)),
        compiler_params=pltpu.CompilerParams(
            dimension_semantics=("parallel",), vmem_limit_bytes=VMEM_LIMIT),
        name="mix_out",
    )(u, u, ysb, u, u, u, x, pool_w, pool_scale, conv_w, conv_b, w_branch, w_out, g_post)


def kernel(x, pre_norm_g, w_in, pool_w, pool_scale, conv_w, conv_b, w_branch, w_out, post_norm_g):
    b, s, d = x.shape
    depth = w_in.shape[0]
    n_in = w_in.shape[2]
    w = pool_scale.shape[1]
    assert n_in == 10 * w + 3 * d and w % LANES == 0 and (10 * w) % d == 0
    assert (b * s) % IN_TM == 0 and n_in % IN_TN == 0 and s % SB_TQ == 0 and s % MIX_TM == 0
    assert s >= SB_WIN and SB_WIN % SB_SUB == 0
    key = jnp.arange(SB_WIN)
    suffix = (key[:, None] > key[None, :]).astype(BF16)
    xf = x.reshape(b * s, d)
    g_pre = pre_norm_g.reshape(depth, 1, d)
    g_post = post_norm_g.reshape(depth, 1, d)
    pool_scale = pool_scale.reshape(depth, 1, w)
    conv_b = conv_b.reshape(depth, 1, w)
    pool_w, w_branch, w_out = (a.astype(BF16) for a in (pool_w, w_branch, w_out))
    for l in range(depth):
        u = _in_proj(l, xf, g_pre, w_in)
        ysb = _stick_break(u.reshape(b, s, n_in), suffix, w, 6 * w, 7 * w, 8 * w, 9 * w)
        xf = _mix_out(l, u, ysb.reshape(b * s, w), xf, s,
                      pool_w, pool_scale, conv_w, conv_b, w_branch, w_out, g_post)
    return xf.reshape(b, s, d)
```

```python
import functools

import jax
import jax.numpy as jnp
from jax import lax
from jax.experimental import pallas as pl
from jax.experimental.pallas import tpu as pltpu

F32 = jnp.float32
BF16 = jnp.bfloat16

LANES = 128
HALO = 16
POOL_WINDOWS = (2, 4, 8, 16)
POOL_GROUP = 128
CONV_K = 3
HEAD_DIM = 64
RMS_EPS = 1e-6

IN_TM = 2048
IN_TN = 1024
SB_TQ = 512
SB_SUB = 64
SB_WIN = 256
MIX_TM = 512
VMEM_LIMIT = 56 * 1024 * 1024

SB_DEAD_MASS = 106.0


def _sigmoid(x):
    return 1.0 / (1.0 + jnp.exp(-x))


def _silu(x):
    return x * _sigmoid(x)


def _softplus(z):
    return jnp.maximum(z, jnp.log(1.0 + jnp.exp(jnp.minimum(z, 44.0))))


def _in_proj_kernel(x_ref, g_ref, w_ref, u_ref, h_ref):
    @pl.when(pl.program_id(1) == 0)
    def _():
        x = x_ref[...]
        ms = jnp.mean(x * x, axis=-1, keepdims=True)
        h_ref[...] = (x * lax.rsqrt(ms + RMS_EPS) * g_ref[...]).astype(BF16)

    u_ref[...] = jnp.dot(h_ref[...], w_ref[...].astype(BF16),
                         preferred_element_type=F32).astype(u_ref.dtype)


def _in_proj(layer, x, g, w):
    m, d = x.shape
    n = w.shape[2]
    return pl.pallas_call(
        _in_proj_kernel,
        out_shape=jax.ShapeDtypeStruct((m, n), BF16),
        grid=(m // IN_TM, n // IN_TN),
        in_specs=[
            pl.BlockSpec((IN_TM, d), lambda i, j: (i, 0)),
            pl.BlockSpec((None, 1, d), lambda i, j: (layer, 0, 0)),
            pl.BlockSpec((None, d, IN_TN), lambda i, j: (layer, 0, j)),
        ],
        out_specs=pl.BlockSpec((IN_TM, IN_TN), lambda i, j: (i, j)),
        scratch_shapes=[pltpu.VMEM((IN_TM, d), BF16)],
        compiler_params=pltpu.CompilerParams(
            dimension_semantics=("parallel", "arbitrary"), vmem_limit_bytes=VMEM_LIMIT),
        name="in_proj",
    )(x, g, w)


def _stick_break_kernel(q_ref, k_ref, v_ref, g_ref, suf_ref, o_ref, acc_ref, run_ref):
    sub, win = SB_SUB, SB_WIN
    n_sub = SB_TQ // sub
    n_pair = q_ref.shape[1] // LANES
    hist = win - sub
    q0 = pl.program_id(1) * SB_TQ

    lane = lax.broadcasted_iota(jnp.int32, (sub, LANES), 1)
    first = lane < HEAD_DIM
    col = lax.broadcasted_iota(jnp.int32, (2 * sub, win), 1)
    rsub = lax.broadcasted_iota(jnp.int32, (2 * sub, win), 0) & (sub - 1)
    scale = jnp.asarray(HEAD_DIM ** -0.5, BF16)
    nt = (((1,), (1,)), ((), ()))

    def stacked_q(s, p):
        q = q_ref[s * sub:(s + 1) * sub, p * LANES:(p + 1) * LANES]
        zero = jnp.zeros_like(q)
        return jnp.concatenate([jnp.where(first, q, zero), jnp.where(first, zero, q)], axis=0) * scale

    def logits(s, p, start):
        kt = k_ref[pl.ds(start, win), p * LANES:(p + 1) * LANES]
        return lax.dot_general(stacked_q(s, p), kt, nt, preferred_element_type=F32)

    def weigh(s, p, start, mask, log_beta, later):
        vt = v_ref[pl.ds(start, win), p * LANES:(p + 1) * LANES]
        w = jnp.where(mask, jnp.exp(log_beta - later), 0.0)
        return jnp.dot(w.astype(BF16), vt, preferred_element_type=F32)

    def key_tile(s, p, start, mask, run):
        z = logits(s, p, start)
        sp = _softplus(z)
        log_beta = z - sp
        sp = jnp.where(mask, sp, 0.0)
        cum = jnp.dot(sp.astype(BF16), suf_ref[...], preferred_element_type=F32)
        return weigh(s, p, start, mask, log_beta, cum + run), cum[:, 0:1] + sp[:, 0:1]

    def emit(s, p, acc):
        rows = slice(s * sub, (s + 1) * sub)
        cols = slice(p * LANES, (p + 1) * LANES)
        out = jnp.where(first, acc[:sub, :], acc[sub:, :])
        o_ref[rows, cols] = (out * _silu(g_ref[rows, cols].astype(F32))).astype(o_ref.dtype)

    win_start = [pl.multiple_of(jnp.maximum(q0 + s * sub - hist, 0), sub) for s in range(n_sub)]
    masks = [col < rsub + (q0 + s * sub - win_start[s]) for s in range(n_sub)]
    chains = [(s, p) for s in range(n_sub) for p in range(n_pair)]
    rows = 2 * sub
    zs = [logits(s, p, win_start[s]) for s, p in chains]
    sps, log_betas = [], []
    for (s, p), z in zip(chains, zs):
        sp = _softplus(z)
        log_betas.append(z - sp)
        sps.append(jnp.where(masks[s], sp, 0.0))
    cum_all = jnp.dot(jnp.concatenate([sp.astype(BF16) for sp in sps], axis=0), suf_ref[...],
                      preferred_element_type=F32)
    accs, masses = {}, {}
    for idx, (s, p) in enumerate(chains):
        cum = cum_all[idx * rows:(idx + 1) * rows]
        accs[s, p] = weigh(s, p, win_start[s], masks[s], log_betas[idx], cum)
        masses[s, p] = cum[:, 0:1] + sps[idx][:, 0:1]
        emit(s, p, accs[s, p])

    def pending(s, mass):
        return jnp.where(win_start[s] > 0, mass, SB_DEAD_MASS)

    least = functools.reduce(jnp.minimum, [pending(s, m) for (s, _), m in masses.items()])
    older_end = win_start[n_sub - 1]

    @pl.when((older_end > 0) & (jnp.min(least) < SB_DEAD_MASS))
    def _():
        for idx, key in enumerate(accs):
            acc_ref[idx] = accs[key]
            run_ref[idx] = jnp.broadcast_to(masses[key], (2 * sub, LANES))

        def live(carry):
            end, least_mass = carry
            return (end > 0) & (least_mass < SB_DEAD_MASS)

        def older(carry):
            end, _ = carry
            start = pl.multiple_of(jnp.maximum(end - win, 0), sub)
            least_mass = None
            for idx, (s, p) in enumerate(accs):
                mask = col + start < jnp.minimum(win_start[s], end)
                run = run_ref[idx]
                pv, mass = key_tile(s, p, start, mask, jnp.concatenate([run] * (win // LANES), axis=1))
                acc_ref[idx] += pv
                run = run + jnp.broadcast_to(mass, run.shape)
                run_ref[idx] = run
                low = jnp.min(pending(s, run))
                least_mass = low if least_mass is None else jnp.minimum(least_mass, low)
            return start, least_mass

        lax.while_loop(live, older, (older_end, jnp.min(least)))
        for idx, (s, p) in enumerate(accs):
            emit(s, p, acc_ref[idx])


def _stick_break(u, suffix, width, q_col, k_col, v_col, g_col):
    b, s, _ = u.shape
    n_chain = (SB_TQ // SB_SUB) * (width // LANES)
    qb, kb, vb, gb = (c // width for c in (q_col, k_col, v_col, g_col))
    return pl.pallas_call(
        _stick_break_kernel,
        out_shape=jax.ShapeDtypeStruct((b, s, width), BF16),
        grid=(b, s // SB_TQ),
        in_specs=[
            pl.BlockSpec((None, SB_TQ, width), lambda bi, i: (bi, i, qb)),
            pl.BlockSpec((None, s, width), lambda bi, i: (bi, 0, kb)),
            pl.BlockSpec((None, s, width), lambda bi, i: (bi, 0, vb)),
            pl.BlockSpec((None, SB_TQ, width), lambda bi, i: (bi, i, gb)),
            pl.BlockSpec(suffix.shape, lambda bi, i: (0, 0)),
        ],
        out_specs=pl.BlockSpec((None, SB_TQ, width), lambda bi, i: (bi, i, 0)),
        scratch_shapes=[pltpu.VMEM((n_chain, 2 * SB_SUB, LANES), F32),
                        pltpu.VMEM((n_chain, 2 * SB_SUB, LANES), F32)],
        compiler_params=pltpu.CompilerParams(
            dimension_semantics=("parallel", "arbitrary"), vmem_limit_bytes=VMEM_LIMIT),
        name="stick_break",
    )(u, u, u, u, suffix)


def _mix_out_kernel(seq_tiles, ua_ref, halo_ref, ysb_ref, m0_ref, m1_ref, m2_ref, x_ref,
                    pw_ref, ps_ref, cw_ref, cb_ref, wb_ref, wo_ref, gp_ref, o_ref):
    tm = x_ref.shape[0]
    w = ysb_ref.shape[1]
    ti = pl.program_id(0) % seq_tiles

    def ext(col):
        head = jnp.where(ti == 0, 0.0, halo_ref[:, col:col + w].astype(F32))
        return jnp.concatenate([head, ua_ref[:, col:col + w].astype(F32)], axis=0)

    def back(a, r):
        return pltpu.roll(a, r, axis=0)

    pos = ti * tm + lax.broadcasted_iota(jnp.int32, (tm, 1), 0)
    v = ext(0)
    mixed = []
    for gi, win in enumerate(POOL_WINDOWS):
        vg = v[:, gi * POOL_GROUP:(gi + 1) * POOL_GROUP]
        ssum = vg
        span = 1
        while span < win:
            ssum = ssum + back(ssum, span)
            span *= 2
        inv_cnt = 1.0 / jnp.minimum(pos + 1, win).astype(F32)
        pooled = ssum[HALO:] * inv_cnt - vg[HALO:]
        mixed.append(jnp.dot(pooled.astype(BF16), pw_ref[gi], preferred_element_type=F32))
    y_pool = jnp.concatenate(mixed, axis=1) * ps_ref[...] * _silu(ua_ref[:, w:2 * w].astype(F32))

    z = ext(4 * w) * ext(2 * w)
    y = cw_ref[0:1, :] * back(z, 2)[HALO:] + cw_ref[1:2, :] * back(z, 1)[HALO:] + cw_ref[2:3, :] * z[HALO:]
    y_conv = (ua_ref[:, 3 * w:4 * w].astype(F32) * (y + cb_ref[...])
              * _silu(ua_ref[:, 5 * w:6 * w].astype(F32)))

    merged = None
    for n, (yb, m_ref) in enumerate(((y_pool.astype(BF16), m0_ref), (y_conv.astype(BF16), m1_ref),
                                     (ysb_ref[...], m2_ref))):
        proj = jnp.dot(yb, wb_ref[n], preferred_element_type=F32)
        gated = _sigmoid(m_ref[...].astype(F32)) * proj
        merged = gated if merged is None else merged + gated

    out = jnp.dot(merged.astype(BF16), wo_ref[...], preferred_element_type=F32)
    ms = jnp.mean(out * out, axis=-1, keepdims=True)
    o_ref[...] = x_ref[...] + out * lax.rsqrt(ms + RMS_EPS) * gp_ref[...]


def _mix_out(layer, u, ysb, x, seq_len, pool_w, pool_scale, conv_w, conv_b, w_branch, w_out, g_post):
    m, d = x.shape
    w = ysb.shape[1]
    tm = MIX_TM
    halo_blocks = tm // HALO
    merge_blk = (10 * w) // d

    def param(a):
        return pl.BlockSpec((None,) + a.shape[1:], lambda i: (layer,) + (0,) * (a.ndim - 1))

    return pl.pallas_call(
        functools.partial(_mix_out_kernel, seq_len // tm),
        out_shape=jax.ShapeDtypeStruct((m, d), F32),
        grid=(m // tm,),
        in_specs=[
            pl.BlockSpec((tm, 6 * w), lambda i: (i, 0)),
            pl.BlockSpec((HALO, 6 * w), lambda i: (jnp.maximum(i * halo_blocks - 1, 0), 0)),
            pl.BlockSpec((tm, w), lambda i: (i, 0)),
            pl.BlockSpec((tm, d), lambda i: (i, merge_blk)),
            pl.BlockSpec((tm, d), lambda i: (i, merge_blk + 1)),
            pl.BlockSpec((tm, d), lambda i: (i, merge_blk + 2)),
            pl.BlockSpec((tm, d), lambda i: (i, 0)),
            param(pool_w), param(pool_scale), param(conv_w), param(conv_b),
            param(w_branch), param(w_out), param(g_post),
        ],
        out_specs=pl.BlockSpec((tm, d), lambda i: (i, 0)),
        compiler_params=pltpu.CompilerParams(
            dimension_semantics=("parallel",), vmem_limit_bytes=VMEM_LIMIT),
        name="mix_out",
    )(u, u, ysb, u, u, u, x, pool_w, pool_scale, conv_w, conv_b, w_branch, w_out, g_post)


def kernel(x, pre_norm_g, w_in, pool_w, pool_scale, conv_w, conv_b, w_branch, w_out, post_norm_g):
    b, s, d = x.shape
    depth = w_in.shape[0]
    n_in = w_in.shape[2]
    w = pool_scale.shape[1]
    assert n_in == 10 * w + 3 * d and w % LANES == 0 and (10 * w) % d == 0
    assert (b * s) % IN_TM == 0 and n_in % IN_TN == 0 and s % SB_TQ == 0 and s % MIX_TM == 0
    assert s >= SB_WIN and SB_WIN % SB_SUB == 0
    key = jnp.arange(SB_WIN)
    suffix = (key[:, None] > key[None, :]).astype(BF16)
    xf = x.reshape(b * s, d)
    g_pre = pre_norm_g.reshape(depth, 1, d)
    g_post = post_norm_g.reshape(depth, 1, d)
    pool_scale = pool_scale.reshape(depth, 1, w)
    conv_b = conv_b.reshape(depth, 1, w)
    pool_w, w_branch, w_out = (a.astype(BF16) for a in (pool_w, w_branch, w_out))
    for l in range(depth):
        u = _in_proj(l, xf, g_pre, w_in)
        ysb = _stick_break(u.reshape(b, s, n_in), suffix, w, 6 * w, 7 * w, 8 * w, 9 * w)
        xf = _mix_out(l, u, ysb.reshape(b * s, w), xf, s,
                      pool_w, pool_scale, conv_w, conv_b, w_branch, w_out, g_post)
    return xf.reshape(b, s, d)
```

```python
import functools

import jax
import jax.numpy as jnp
from jax import lax
from jax.experimental import pallas as pl
from jax.experimental.pallas import tpu as pltpu

F32 = jnp.float32
BF16 = jnp.bfloat16

LANES = 128
HALO = 16
POOL_WINDOWS = (2, 4, 8, 16)
POOL_GROUP = 128
CONV_K = 3
HEAD_DIM = 64
RMS_EPS = 1e-6
LOG2E = 1.4426950408889634

IN_TM = 2048
IN_TN = 1024
SB_TQ = 512
SB_SUB = 64
SB_WIN = 256
MIX_TM = 512
VMEM_LIMIT = 56 * 1024 * 1024

SB_DEAD_MASS = 106.0


def _sigmoid(x):
    return 1.0 / (1.0 + jnp.exp2(x * -LOG2E))


def _silu(x):
    return x * _sigmoid(x)


def _softplus(z):
    return jnp.maximum(z, jnp.log(1.0 + jnp.exp(jnp.minimum(z, 44.0))))


def _in_proj_kernel(x_ref, g_ref, w_ref, u_ref, h_ref):
    @pl.when(pl.program_id(1) == 0)
    def _():
        x = x_ref[...]
        ms = jnp.mean(x * x, axis=-1, keepdims=True)
        h_ref[...] = (x * lax.rsqrt(ms + RMS_EPS) * g_ref[...]).astype(BF16)

    u_ref[...] = jnp.dot(h_ref[...], w_ref[...].astype(BF16),
                         preferred_element_type=F32).astype(u_ref.dtype)


def _in_proj(layer, x, g, w):
    m, d = x.shape
    n = w.shape[2]
    return pl.pallas_call(
        _in_proj_kernel,
        out_shape=jax.ShapeDtypeStruct((m, n), BF16),
        grid=(m // IN_TM, n // IN_TN),
        in_specs=[
            pl.BlockSpec((IN_TM, d), lambda i, j: (i, 0)),
            pl.BlockSpec((None, 1, d), lambda i, j: (layer, 0, 0)),
            pl.BlockSpec((None, d, IN_TN), lambda i, j: (layer, 0, j)),
        ],
        out_specs=pl.BlockSpec((IN_TM, IN_TN), lambda i, j: (i, j)),
        scratch_shapes=[pltpu.VMEM((IN_TM, d), BF16)],
        compiler_params=pltpu.CompilerParams(
            dimension_semantics=("parallel", "arbitrary"), vmem_limit_bytes=VMEM_LIMIT),
        name="in_proj",
    )(x, g, w)


def _stick_break_kernel(q_ref, k_ref, v_ref, g_ref, suf_ref, o_ref, acc_ref, run_ref):
    sub, win = SB_SUB, SB_WIN
    n_sub = SB_TQ // sub
    n_pair = q_ref.shape[1] // LANES
    hist = win - sub
    q0 = pl.program_id(1) * SB_TQ

    lane = lax.broadcasted_iota(jnp.int32, (sub, LANES), 1)
    first = lane < HEAD_DIM
    col = lax.broadcasted_iota(jnp.int32, (2 * sub, win), 1)
    rsub = lax.broadcasted_iota(jnp.int32, (2 * sub, win), 0) & (sub - 1)
    scale = jnp.asarray(HEAD_DIM ** -0.5, BF16)
    nt = (((1,), (1,)), ((), ()))

    def stacked_q(s, p):
        q = q_ref[s * sub:(s + 1) * sub, p * LANES:(p + 1) * LANES]
        zero = jnp.zeros_like(q)
        return jnp.concatenate([jnp.where(first, q, zero), jnp.where(first, zero, q)], axis=0) * scale

    def logits(s, p, start):
        kt = k_ref[pl.ds(start, win), p * LANES:(p + 1) * LANES]
        return lax.dot_general(stacked_q(s, p), kt, nt, preferred_element_type=F32)

    def weigh(s, p, start, mask, log_beta, later):
        vt = v_ref[pl.ds(start, win), p * LANES:(p + 1) * LANES]
        w = jnp.where(mask, jnp.exp(log_beta - later), 0.0)
        return jnp.dot(w.astype(BF16), vt, preferred_element_type=F32)

    def key_tile(s, p, start, mask, run):
        z = logits(s, p, start)
        sp = _softplus(z)
        log_beta = z - sp
        sp = jnp.where(mask, sp, 0.0)
        cum = jnp.dot(sp.astype(BF16), suf_ref[...], preferred_element_type=F32)
        return weigh(s, p, start, mask, log_beta, cum + run), cum[:, 0:1] + sp[:, 0:1]

    def emit(s, p, acc):
        rows = slice(s * sub, (s + 1) * sub)
        cols = slice(p * LANES, (p + 1) * LANES)
        out = jnp.where(first, acc[:sub, :], acc[sub:, :])
        o_ref[rows, cols] = (out * _silu(g_ref[rows, cols].astype(F32))).astype(o_ref.dtype)

    win_start = [pl.multiple_of(jnp.maximum(q0 + s * sub - hist, 0), sub) for s in range(n_sub)]
    masks = [col < rsub + (q0 + s * sub - win_start[s]) for s in range(n_sub)]
    chains = [(s, p) for s in range(n_sub) for p in range(n_pair)]
    rows = 2 * sub
    zs = [logits(s, p, win_start[s]) for s, p in chains]
    sps, log_betas = [], []
    for (s, p), z in zip(chains, zs):
        sp = _softplus(z)
        log_betas.append(z - sp)
        sps.append(jnp.where(masks[s], sp, 0.0))
    cum_all = jnp.dot(jnp.concatenate([sp.astype(BF16) for sp in sps], axis=0), suf_ref[...],
                      preferred_element_type=F32)
    accs, masses = {}, {}
    for idx, (s, p) in enumerate(chains):
        cum = cum_all[idx * rows:(idx + 1) * rows]
        accs[s, p] = weigh(s, p, win_start[s], masks[s], log_betas[idx], cum)
        masses[s, p] = cum[:, 0:1] + sps[idx][:, 0:1]
        emit(s, p, accs[s, p])

    def pending(s, mass):
        return jnp.where(win_start[s] > 0, mass, SB_DEAD_MASS)

    least = functools.reduce(jnp.minimum, [pending(s, m) for (s, _), m in masses.items()])
    older_end = win_start[n_sub - 1]

    @pl.when((older_end > 0) & (jnp.min(least) < SB_DEAD_MASS))
    def _():
        for idx, key in enumerate(accs):
            acc_ref[idx] = accs[key]
            run_ref[idx] = jnp.broadcast_to(masses[key], (2 * sub, LANES))

        def live(carry):
            end, least_mass = carry
            return (end > 0) & (least_mass < SB_DEAD_MASS)

        def older(carry):
            end, _ = carry
            start = pl.multiple_of(jnp.maximum(end - win, 0), sub)
            least_mass = None
            for idx, (s, p) in enumerate(accs):
                mask = col + start < jnp.minimum(win_start[s], end)
                run = run_ref[idx]
                pv, mass = key_tile(s, p, start, mask, jnp.concatenate([run] * (win // LANES), axis=1))
                acc_ref[idx] += pv
                run = run + jnp.broadcast_to(mass, run.shape)
                run_ref[idx] = run
                low = jnp.min(pending(s, run))
                least_mass = low if least_mass is None else jnp.minimum(least_mass, low)
            return start, least_mass

        lax.while_loop(live, older, (older_end, jnp.min(least)))
        for idx, (s, p) in enumerate(accs):
            emit(s, p, acc_ref[idx])


def _stick_break(u, suffix, width, q_col, k_col, v_col, g_col):
    b, s, _ = u.shape
    n_chain = (SB_TQ // SB_SUB) * (width // LANES)
    qb, kb, vb, gb = (c // width for c in (q_col, k_col, v_col, g_col))
    return pl.pallas_call(
        _stick_break_kernel,
        out_shape=jax.ShapeDtypeStruct((b, s, width), BF16),
        grid=(b, s // SB_TQ),
        in_specs=[
            pl.BlockSpec((None, SB_TQ, width), lambda bi, i: (bi, i, qb)),
            pl.BlockSpec((None, s, width), lambda bi, i: (bi, 0, kb)),
            pl.BlockSpec((None, s, width), lambda bi, i: (bi, 0, vb)),
            pl.BlockSpec((None, SB_TQ, width), lambda bi, i: (bi, i, gb)),
            pl.BlockSpec(suffix.shape, lambda bi, i: (0, 0)),
        ],
        out_specs=pl.BlockSpec((None, SB_TQ, width), lambda bi, i: (bi, i, 0)),
        scratch_shapes=[pltpu.VMEM((n_chain, 2 * SB_SUB, LANES), F32),
                        pltpu.VMEM((n_chain, 2 * SB_SUB, LANES), F32)],
        compiler_params=pltpu.CompilerParams(
            dimension_semantics=("parallel", "arbitrary"), vmem_limit_bytes=VMEM_LIMIT),
        name="stick_break",
    )(u, u, u, u, suffix)


def _mix_out_kernel(seq_tiles, ua_ref, halo_ref, ysb_ref, m0_ref, m1_ref, m2_ref, x_ref,
                    pw_ref, ps_ref, cw_ref, cb_ref, wb_ref, wo_ref, gp_ref, o_ref):
    tm = x_ref.shape[0]
    w = ysb_ref.shape[1]
    ti = pl.program_id(0) % seq_tiles

    def ext(col):
        head = jnp.where(ti == 0, 0.0, halo_ref[:, col:col + w].astype(F32))
        return jnp.concatenate([head, ua_ref[:, col:col + w].astype(F32)], axis=0)

    def back(a, r):
        return pltpu.roll(a, r, axis=0)

    pos = ti * tm + lax.broadcasted_iota(jnp.int32, (tm, 1), 0)
    v = ext(0)
    mixed = []
    for gi, win in enumerate(POOL_WINDOWS):
        vg = v[:, gi * POOL_GROUP:(gi + 1) * POOL_GROUP]
        ssum = vg
        span = 1
        while span < win:
            ssum = ssum + back(ssum, span)
            span *= 2
        inv_cnt = 1.0 / jnp.minimum(pos + 1, win).astype(F32)
        pooled = ssum[HALO:] * inv_cnt - vg[HALO:]
        mixed.append(jnp.dot(pooled.astype(BF16), pw_ref[gi], preferred_element_type=F32))
    y_pool = jnp.concatenate(mixed, axis=1) * ps_ref[...] * _silu(ua_ref[:, w:2 * w].astype(F32))

    z = ext(4 * w) * ext(2 * w)
    y = cw_ref[0:1, :] * back(z, 2)[HALO:] + cw_ref[1:2, :] * back(z, 1)[HALO:] + cw_ref[2:3, :] * z[HALO:]
    y_conv = (ua_ref[:, 3 * w:4 * w].astype(F32) * (y + cb_ref[...])
              * _silu(ua_ref[:, 5 * w:6 * w].astype(F32)))

    merged = None
    for n, (yb, m_ref) in enumerate(((y_pool.astype(BF16), m0_ref), (y_conv.astype(BF16), m1_ref),
                                     (ysb_ref[...], m2_ref))):
        proj = jnp.dot(yb, wb_ref[n], preferred_element_type=F32)
        gated = _sigmoid(m_ref[...].astype(F32)) * proj
        merged = gated if merged is None else merged + gated

    out = jnp.dot(merged.astype(BF16), wo_ref[...], preferred_element_type=F32)
    ms = jnp.mean(out * out, axis=-1, keepdims=True)
    o_ref[...] = x_ref[...] + out * lax.rsqrt(ms + RMS_EPS) * gp_ref[...]


def _mix_out(layer, u, ysb, x, seq_len, pool_w, pool_scale, conv_w, conv_b, w_branch, w_out, g_post):
    m, d = x.shape
    w = ysb.shape[1]
    tm = MIX_TM
    halo_blocks = tm // HALO
    merge_blk = (10 * w) // d

    def param(a):
        return pl.BlockSpec((None,) + a.shape[1:], lambda i: (layer,) + (0,) * (a.ndim - 1))

    return pl.pallas_call(
        functools.partial(_mix_out_kernel, seq_len // tm),
        out_shape=jax.ShapeDtypeStruct((m, d), F32),
        grid=(m // tm,),
        in_specs=[
            pl.BlockSpec((tm, 6 * w), lambda i: (i, 0)),
            pl.BlockSpec((HALO, 6 * w), lambda i: (jnp.maximum(i * halo_blocks - 1, 0), 0)),
            pl.BlockSpec((tm, w), lambda i: (i, 0)),
            pl.BlockSpec((tm, d), lambda i: (i, merge_blk)),
            pl.BlockSpec((tm, d), lambda i: (i, merge_blk + 1)),
            pl.BlockSpec((tm, d), lambda i: (i, merge_blk + 2)),
            pl.BlockSpec((tm, d), lambda i: (i, 0)),
            param(pool_w), param(pool_scale), param(conv_w), param(conv_b),
            param(w_branch), param(w_out), param(g_post),
        ],
        out_specs=pl.BlockSpec((tm, d), lambda i: (i, 0)),
        compiler_params=pltpu.CompilerParams(
            dimension_semantics=("parallel",), vmem_limit_bytes=VMEM_LIMIT),
        name="mix_out",
    )(u, u, ysb, u, u, u, x, pool_w, pool_scale, conv_w, conv_b, w_branch, w_out, g_post)


def kernel(x, pre_norm_g, w_in, pool_w, pool_scale, conv_w, conv_b, w_branch, w_out, post_norm_g):
    b, s, d = x.shape
    depth = w_in.shape[0]
    n_in = w_in.shape[2]
    w = pool_scale.shape[1]
    assert n_in == 10 * w + 3 * d and w % LANES == 0 and (10 * w) % d == 0
    assert (b * s) % IN_TM == 0 and n_in % IN_TN == 0 and s % SB_TQ == 0 and s % MIX_TM == 0
    assert s >= SB_WIN and SB_WIN % SB_SUB == 0
    key = jnp.arange(SB_WIN)
    suffix = (key[:, None] > key[None, :]).astype(BF16)
    xf = x.reshape(b * s, d)
    g_pre = pre_norm_g.reshape(depth, 1, d)
    g_post = post_norm_g.reshape(depth, 1, d)
    pool_scale = pool_scale.reshape(depth, 1, w)
    conv_b = conv_b.reshape(depth, 1, w)
    pool_w, w_branch, w_out = (a.astype(BF16) for a in (pool_w, w_branch, w_out))
    for l in range(depth):
        u = _in_proj(l, xf, g_pre, w_in)
        ysb = _stick_break(u.reshape(b, s, n_in), suffix, w, 6 * w, 7 * w, 8 * w, 9 * w)
        xf = _mix_out(l, u, ysb.reshape(b * s, w), xf, s,
                      pool_w, pool_scale, conv_w, conv_b, w_branch, w_out, g_post)
    return xf.reshape(b, s, d)
```

```python
import functools

import jax
import jax.numpy as jnp
from jax import lax
from jax.experimental import pallas as pl
from jax.experimental.pallas import tpu as pltpu

F32 = jnp.float32
BF16 = jnp.bfloat16

LANES = 128
HALO = 16
POOL_WINDOWS = (2, 4, 8, 16)
POOL_GROUP = 128
CONV_K = 3
HEAD_DIM = 64
RMS_EPS = 1e-6
LOG2E = 1.4426950408889634

IN_TM = 2048
IN_TN = 1024
SB_TQ = 1024
SB_SUB = 64
SB_WIN = 256
MIX_TM = 512
VMEM_LIMIT = 56 * 1024 * 1024

SB_DEAD_MASS = 106.0


def _sigmoid(x):
    return 1.0 / (1.0 + jnp.exp2(x * -LOG2E))


def _silu(x):
    return x * _sigmoid(x)


def _softplus(z):
    return jnp.maximum(z, jnp.log(1.0 + jnp.exp(jnp.minimum(z, 44.0))))


def _in_proj_kernel(x_ref, g_ref, w_ref, u_ref, h_ref):
    @pl.when(pl.program_id(1) == 0)
    def _():
        x = x_ref[...]
        ms = jnp.mean(x * x, axis=-1, keepdims=True)
        h_ref[...] = (x * lax.rsqrt(ms + RMS_EPS) * g_ref[...]).astype(BF16)

    u_ref[...] = jnp.dot(h_ref[...], w_ref[...].astype(BF16),
                         preferred_element_type=F32).astype(u_ref.dtype)


def _in_proj(layer, x, g, w):
    m, d = x.shape
    n = w.shape[2]
    return pl.pallas_call(
        _in_proj_kernel,
        out_shape=jax.ShapeDtypeStruct((m, n), BF16),
        grid=(m // IN_TM, n // IN_TN),
        in_specs=[
            pl.BlockSpec((IN_TM, d), lambda i, j: (i, 0)),
            pl.BlockSpec((None, 1, d), lambda i, j: (layer, 0, 0)),
            pl.BlockSpec((None, d, IN_TN), lambda i, j: (layer, 0, j)),
        ],
        out_specs=pl.BlockSpec((IN_TM, IN_TN), lambda i, j: (i, j)),
        scratch_shapes=[pltpu.VMEM((IN_TM, d), BF16)],
        compiler_params=pltpu.CompilerParams(
            dimension_semantics=("parallel", "arbitrary"), vmem_limit_bytes=VMEM_LIMIT),
        name="in_proj",
    )(x, g, w)


def _stick_break_kernel(q_ref, k_ref, v_ref, g_ref, suf_ref, o_ref, acc_ref, run_ref):
    sub, win = SB_SUB, SB_WIN
    n_sub = SB_TQ // sub
    n_pair = q_ref.shape[1] // LANES
    hist = win - sub
    q0 = pl.program_id(1) * SB_TQ

    lane = lax.broadcasted_iota(jnp.int32, (sub, LANES), 1)
    first = lane < HEAD_DIM
    col = lax.broadcasted_iota(jnp.int32, (2 * sub, win), 1)
    rsub = lax.broadcasted_iota(jnp.int32, (2 * sub, win), 0) & (sub - 1)
    scale = jnp.asarray(HEAD_DIM ** -0.5, BF16)
    nt = (((1,), (1,)), ((), ()))

    def stacked_q(s, p):
        q = q_ref[s * sub:(s + 1) * sub, p * LANES:(p + 1) * LANES]
        zero = jnp.zeros_like(q)
        return jnp.concatenate([jnp.where(first, q, zero), jnp.where(first, zero, q)], axis=0) * scale

    def logits(s, p, start):
        kt = k_ref[pl.ds(start, win), p * LANES:(p + 1) * LANES]
        return lax.dot_general(stacked_q(s, p), kt, nt, preferred_element_type=F32)

    def weigh(s, p, start, mask, log_beta, later):
        vt = v_ref[pl.ds(start, win), p * LANES:(p + 1) * LANES]
        w = jnp.where(mask, jnp.exp(log_beta - later), 0.0)
        return jnp.dot(w.astype(BF16), vt, preferred_element_type=F32)

    def key_tile(s, p, start, mask, run):
        z = logits(s, p, start)
        sp = _softplus(z)
        log_beta = z - sp
        sp = jnp.where(mask, sp, 0.0)
        cum = jnp.dot(sp.astype(BF16), suf_ref[...], preferred_element_type=F32)
        return weigh(s, p, start, mask, log_beta, cum + run), cum[:, 0:1] + sp[:, 0:1]

    def emit(s, p, acc):
        rows = slice(s * sub, (s + 1) * sub)
        cols = slice(p * LANES, (p + 1) * LANES)
        out = jnp.where(first, acc[:sub, :], acc[sub:, :])
        o_ref[rows, cols] = (out * _silu(g_ref[rows, cols].astype(F32))).astype(o_ref.dtype)

    win_start = [pl.multiple_of(jnp.maximum(q0 + s * sub - hist, 0), sub) for s in range(n_sub)]
    masks = [col < rsub + (q0 + s * sub - win_start[s]) for s in range(n_sub)]
    chains = [(s, p) for s in range(n_sub) for p in range(n_pair)]
    rows = 2 * sub
    zs = [logits(s, p, win_start[s]) for s, p in chains]
    sps, log_betas = [], []
    for (s, p), z in zip(chains, zs):
        sp = _softplus(z)
        log_betas.append(z - sp)
        sps.append(jnp.where(masks[s], sp, 0.0))
    cum_all = jnp.dot(jnp.concatenate([sp.astype(BF16) for sp in sps], axis=0), suf_ref[...],
                      preferred_element_type=F32)
    accs, masses = {}, {}
    for idx, (s, p) in enumerate(chains):
        cum = cum_all[idx * rows:(idx + 1) * rows]
        accs[s, p] = weigh(s, p, win_start[s], masks[s], log_betas[idx], cum)
        masses[s, p] = cum[:, 0:1] + sps[idx][:, 0:1]
        emit(s, p, accs[s, p])

    def pending(s, mass):
        return jnp.where(win_start[s] > 0, mass, SB_DEAD_MASS)

    least = functools.reduce(jnp.minimum, [pending(s, m) for (s, _), m in masses.items()])
    older_end = win_start[n_sub - 1]

    @pl.when((older_end > 0) & (jnp.min(least) < SB_DEAD_MASS))
    def _():
        for idx, key in enumerate(accs):
            acc_ref[idx] = accs[key]
            run_ref[idx] = jnp.broadcast_to(masses[key], (2 * sub, LANES))

        def live(carry):
            end, least_mass = carry
            return (end > 0) & (least_mass < SB_DEAD_MASS)

        def older(carry):
            end, _ = carry
            start = pl.multiple_of(jnp.maximum(end - win, 0), sub)
            least_mass = None
            for idx, (s, p) in enumerate(accs):
                mask = col + start < jnp.minimum(win_start[s], end)
                run = run_ref[idx]
                pv, mass = key_tile(s, p, start, mask, jnp.concatenate([run] * (win // LANES), axis=1))
                acc_ref[idx] += pv
                run = run + jnp.broadcast_to(mass, run.shape)
                run_ref[idx] = run
                low = jnp.min(pending(s, run))
                least_mass = low if least_mass is None else jnp.minimum(least_mass, low)
            return start, least_mass

        lax.while_loop(live, older, (older_end, jnp.min(least)))
        for idx, (s, p) in enumerate(accs):
            emit(s, p, acc_ref[idx])


def _stick_break(u, suffix, width, q_col, k_col, v_col, g_col):
    b, s, _ = u.shape
    n_chain = (SB_TQ // SB_SUB) * (width // LANES)
    qb, kb, vb, gb = (c // width for c in (q_col, k_col, v_col, g_col))
    return pl.pallas_call(
        _stick_break_kernel,
        out_shape=jax.ShapeDtypeStruct((b, s, width), BF16),
        grid=(b, s // SB_TQ),
        in_specs=[
            pl.BlockSpec((None, SB_TQ, width), lambda bi, i: (bi, i, qb)),
            pl.BlockSpec((None, s, width), lambda bi, i: (bi, 0, kb)),
            pl.BlockSpec((None, s, width), lambda bi, i: (bi, 0, vb)),
            pl.BlockSpec((None, SB_TQ, width), lambda bi, i: (bi, i, gb)),
            pl.BlockSpec(suffix.shape, lambda bi, i: (0, 0)),
        ],
        out_specs=pl.BlockSpec((None, SB_TQ, width), lambda bi, i: (bi, i, 0)),
        scratch_shapes=[pltpu.VMEM((n_chain, 2 * SB_SUB, LANES), F32),
                        pltpu.VMEM((n_chain, 2 * SB_SUB, LANES), F32)],
        compiler_params=pltpu.CompilerParams(
            dimension_semantics=("parallel", "arbitrary"), vmem_limit_bytes=VMEM_LIMIT),
        name="stick_break",
    )(u, u, u, u, suffix)


def _mix_out_kernel(seq_tiles, ua_ref, halo_ref, ysb_ref, m0_ref, m1_ref, m2_ref, x_ref,
                    pw_ref, ps_ref, cw_ref, cb_ref, wb_ref, wo_ref, gp_ref, o_ref):
    tm = x_ref.shape[0]
    w = ysb_ref.shape[1]
    ti = pl.program_id(0) % seq_tiles

    def ext(col):
        head = jnp.where(ti == 0, 0.0, halo_ref[:, col:col + w].astype(F32))
        return jnp.concatenate([head, ua_ref[:, col:col + w].astype(F32)], axis=0)

    def back(a, r):
        return pltpu.roll(a, r, axis=0)

    pos = ti * tm + lax.broadcasted_iota(jnp.int32, (tm, 1), 0)
    v = ext(0)
    mixed = []
    for gi, win in enumerate(POOL_WINDOWS):
        vg = v[:, gi * POOL_GROUP:(gi + 1) * POOL_GROUP]
        ssum = vg
        span = 1
        while span < win:
            ssum = ssum + back(ssum, span)
            span *= 2
        inv_cnt = 1.0 / jnp.minimum(pos + 1, win).astype(F32)
        pooled = ssum[HALO:] * inv_cnt - vg[HALO:]
        mixed.append(jnp.dot(pooled.astype(BF16), pw_ref[gi], preferred_element_type=F32))
    y_pool = jnp.concatenate(mixed, axis=1) * ps_ref[...] * _silu(ua_ref[:, w:2 * w].astype(F32))

    z = ext(4 * w) * ext(2 * w)
    y = cw_ref[0:1, :] * back(z, 2)[HALO:] + cw_ref[1:2, :] * back(z, 1)[HALO:] + cw_ref[2:3, :] * z[HALO:]
    y_conv = (ua_ref[:, 3 * w:4 * w].astype(F32) * (y + cb_ref[...])
              * _silu(ua_ref[:, 5 * w:6 * w].astype(F32)))

    merged = None
    for n, (yb, m_ref) in enumerate(((y_pool.astype(BF16), m0_ref), (y_conv.astype(BF16), m1_ref),
                                     (ysb_ref[...], m2_ref))):
        proj = jnp.dot(yb, wb_ref[n], preferred_element_type=F32)
        gated = _sigmoid(m_ref[...].astype(F32)) * proj
        merged = gated if merged is None else merged + gated

    out = jnp.dot(merged.astype(BF16), wo_ref[...], preferred_element_type=F32)
    ms = jnp.mean(out * out, axis=-1, keepdims=True)
    o_ref[...] = x_ref[...] + out * lax.rsqrt(ms + RMS_EPS) * gp_ref[...]


def _mix_out(layer, u, ysb, x, seq_len, pool_w, pool_scale, conv_w, conv_b, w_branch, w_out, g_post):
    m, d = x.shape
    w = ysb.shape[1]
    tm = MIX_TM
    halo_blocks = tm // HALO
    merge_blk = (10 * w) // d

    def param(a):
        return pl.BlockSpec((None,) + a.shape[1:], lambda i: (layer,) + (0,) * (a.ndim - 1))

    return pl.pallas_call(
        functools.partial(_mix_out_kernel, seq_len // tm),
        out_shape=jax.ShapeDtypeStruct((m, d), F32),
        grid=(m // tm,),
        in_specs=[
            pl.BlockSpec((tm, 6 * w), lambda i: (i, 0)),
            pl.BlockSpec((HALO, 6 * w), lambda i: (jnp.maximum(i * halo_blocks - 1, 0), 0)),
            pl.BlockSpec((tm, w), lambda i: (i, 0)),
            pl.BlockSpec((tm, d), lambda i: (i, merge_blk)),
            pl.BlockSpec((tm, d), lambda i: (i, merge_blk + 1)),
            pl.BlockSpec((tm, d), lambda i: (i, merge_blk + 2)),
            pl.BlockSpec((tm, d), lambda i: (i, 0)),
            param(pool_w), param(pool_scale), param(conv_w), param(conv_b),
            param(w_branch), param(w_out), param(g_post),
        ],
        out_specs=pl.BlockSpec((tm, d), lambda i: (i, 0)),
        compiler_params=pltpu.CompilerParams(
            dimension_semantics=("parallel",), vmem_limit_bytes=VMEM_LIMIT),
        name="mix_out",
    )(u, u, ysb, u, u, u, x, pool_w, pool_scale, conv_w, conv_b, w_branch, w_out, g_post)


def kernel(x, pre_norm_g, w_in, pool_w, pool_scale, conv_w, conv_b, w_branch, w_out, post_norm_g):
    b, s, d = x.shape
    depth = w_in.shape[0]
    n_in = w_in.shape[2]
    w = pool_scale.shape[1]
    assert n_in == 10 * w + 3 * d and w % LANES == 0 and (10 * w) % d == 0
    assert (b * s) % IN_TM == 0 and n_in % IN_TN == 0 and s % SB_TQ == 0 and s % MIX_TM == 0
    assert s >= SB_WIN and SB_WIN % SB_SUB == 0
    key = jnp.arange(SB_WIN)
    suffix = (key[:, None] > key[None, :]).astype(BF16)
    xf = x.reshape(b * s, d)
    g_pre = pre_norm_g.reshape(depth, 1, d)
    g_post = post_norm_g.reshape(depth, 1, d)
    pool_scale = pool_scale.reshape(depth, 1, w)
    conv_b = conv_b.reshape(depth, 1, w)
    pool_w, w_branch, w_out = (a.astype(BF16) for a in (pool_w, w_branch, w_out))
    for l in range(depth):
        u = _in_proj(l, xf, g_pre, w_in)
        ysb = _stick_break(u.reshape(b, s, n_in), suffix, w, 6 * w, 7 * w, 8 * w, 9 * w)
        xf = _mix_out(l, u, ysb.reshape(b * s, w), xf, s,
                      pool_w, pool_scale, conv_w, conv_b, w_branch, w_out, g_post)
    return xf.reshape(b, s, d)
```

```python
import functools

import jax
import jax.numpy as jnp
from jax import lax
from jax.experimental import pallas as pl
from jax.experimental.pallas import tpu as pltpu

F32 = jnp.float32
BF16 = jnp.bfloat16

LANES = 128
HALO = 16
POOL_WINDOWS = (2, 4, 8, 16)
POOL_GROUP = 128
CONV_K = 3
HEAD_DIM = 64
RMS_EPS = 1e-6
LOG2E = 1.4426950408889634

IN_TM = 2048
IN_TN = 1024
SB_TQ = 1024
SB_SUB = 64
SB_WIN = 256
MIX_TM = 1024
VMEM_LIMIT = 56 * 1024 * 1024

SB_DEAD_MASS = 106.0


def _sigmoid(x):
    return 1.0 / (1.0 + jnp.exp2(x * -LOG2E))


def _silu(x):
    return x * _sigmoid(x)


def _softplus(z):
    return jnp.maximum(z, jnp.log(1.0 + jnp.exp(jnp.minimum(z, 44.0))))


def _in_proj_kernel(x_ref, g_ref, w_ref, u_ref, h_ref):
    @pl.when(pl.program_id(1) == 0)
    def _():
        x = x_ref[...]
        ms = jnp.mean(x * x, axis=-1, keepdims=True)
        h_ref[...] = (x * lax.rsqrt(ms + RMS_EPS) * g_ref[...]).astype(BF16)

    u_ref[...] = jnp.dot(h_ref[...], w_ref[...].astype(BF16),
                         preferred_element_type=F32).astype(u_ref.dtype)


def _in_proj(layer, x, g, w):
    m, d = x.shape
    n = w.shape[2]
    return pl.pallas_call(
        _in_proj_kernel,
        out_shape=jax.ShapeDtypeStruct((m, n), BF16),
        grid=(m // IN_TM, n // IN_TN),
        in_specs=[
            pl.BlockSpec((IN_TM, d), lambda i, j: (i, 0)),
            pl.BlockSpec((None, 1, d), lambda i, j: (layer, 0, 0)),
            pl.BlockSpec((None, d, IN_TN), lambda i, j: (layer, 0, j)),
        ],
        out_specs=pl.BlockSpec((IN_TM, IN_TN), lambda i, j: (i, j)),
        scratch_shapes=[pltpu.VMEM((IN_TM, d), BF16)],
        compiler_params=pltpu.CompilerParams(
            dimension_semantics=("parallel", "arbitrary"), vmem_limit_bytes=VMEM_LIMIT),
        name="in_proj",
    )(x, g, w)


def _stick_break_kernel(q_ref, k_ref, v_ref, g_ref, suf_ref, o_ref, acc_ref, run_ref):
    sub, win = SB_SUB, SB_WIN
    n_sub = SB_TQ // sub
    n_pair = q_ref.shape[1] // LANES
    hist = win - sub
    q0 = pl.program_id(1) * SB_TQ

    lane = lax.broadcasted_iota(jnp.int32, (sub, LANES), 1)
    first = lane < HEAD_DIM
    col = lax.broadcasted_iota(jnp.int32, (2 * sub, win), 1)
    rsub = lax.broadcasted_iota(jnp.int32, (2 * sub, win), 0) & (sub - 1)
    scale = jnp.asarray(HEAD_DIM ** -0.5, BF16)
    nt = (((1,), (1,)), ((), ()))

    def stacked_q(s, p):
        q = q_ref[s * sub:(s + 1) * sub, p * LANES:(p + 1) * LANES]
        zero = jnp.zeros_like(q)
        return jnp.concatenate([jnp.where(first, q, zero), jnp.where(first, zero, q)], axis=0) * scale

    def logits(s, p, start):
        kt = k_ref[pl.ds(start, win), p * LANES:(p + 1) * LANES]
        return lax.dot_general(stacked_q(s, p), kt, nt, preferred_element_type=F32)

    def weigh(s, p, start, mask, log_beta, later):
        vt = v_ref[pl.ds(start, win), p * LANES:(p + 1) * LANES]
        w = jnp.where(mask, jnp.exp(log_beta - later), 0.0)
        return jnp.dot(w.astype(BF16), vt, preferred_element_type=F32)

    def key_tile(s, p, start, mask, run):
        z = logits(s, p, start)
        sp = _softplus(z)
        log_beta = z - sp
        sp = jnp.where(mask, sp, 0.0)
        cum = jnp.dot(sp.astype(BF16), suf_ref[...], preferred_element_type=F32)
        return weigh(s, p, start, mask, log_beta, cum + run), cum[:, 0:1] + sp[:, 0:1]

    def emit(s, p, acc):
        rows = slice(s * sub, (s + 1) * sub)
        cols = slice(p * LANES, (p + 1) * LANES)
        out = jnp.where(first, acc[:sub, :], acc[sub:, :])
        o_ref[rows, cols] = (out * _silu(g_ref[rows, cols].astype(F32))).astype(o_ref.dtype)

    win_start = [pl.multiple_of(jnp.maximum(q0 + s * sub - hist, 0), sub) for s in range(n_sub)]
    masks = [col < rsub + (q0 + s * sub - win_start[s]) for s in range(n_sub)]
    chains = [(s, p) for s in range(n_sub) for p in range(n_pair)]
    rows = 2 * sub
    zs = [logits(s, p, win_start[s]) for s, p in chains]
    sps, log_betas = [], []
    for (s, p), z in zip(chains, zs):
        sp = _softplus(z)
        log_betas.append(z - sp)
        sps.append(jnp.where(masks[s], sp, 0.0))
    cum_all = jnp.dot(jnp.concatenate([sp.astype(BF16) for sp in sps], axis=0), suf_ref[...],
                      preferred_element_type=F32)
    accs, masses = {}, {}
    for idx, (s, p) in enumerate(chains):
        cum = cum_all[idx * rows:(idx + 1) * rows]
        accs[s, p] = weigh(s, p, win_start[s], masks[s], log_betas[idx], cum)
        masses[s, p] = cum[:, 0:1] + sps[idx][:, 0:1]
        emit(s, p, accs[s, p])

    def pending(s, mass):
        return jnp.where(win_start[s] > 0, mass, SB_DEAD_MASS)

    least = functools.reduce(jnp.minimum, [pending(s, m) for (s, _), m in masses.items()])
    older_end = win_start[n_sub - 1]

    @pl.when((older_end > 0) & (jnp.min(least) < SB_DEAD_MASS))
    def _():
        for idx, key in enumerate(accs):
            acc_ref[idx] = accs[key]
            run_ref[idx] = jnp.broadcast_to(masses[key], (2 * sub, LANES))

        def live(carry):
            end, least_mass = carry
            return (end > 0) & (least_mass < SB_DEAD_MASS)

        def older(carry):
            end, _ = carry
            start = pl.multiple_of(jnp.maximum(end - win, 0), sub)
            least_mass = None
            for idx, (s, p) in enumerate(accs):
                mask = col + start < jnp.minimum(win_start[s], end)
                run = run_ref[idx]
                pv, mass = key_tile(s, p, start, mask, jnp.concatenate([run] * (win // LANES), axis=1))
                acc_ref[idx] += pv
                run = run + jnp.broadcast_to(mass, run.shape)
                run_ref[idx] = run
                low = jnp.min(pending(s, run))
                least_mass = low if least_mass is None else jnp.minimum(least_mass, low)
            return start, least_mass

        lax.while_loop(live, older, (older_end, jnp.min(least)))
        for idx, (s, p) in enumerate(accs):
            emit(s, p, acc_ref[idx])


def _stick_break(u, suffix, width, q_col, k_col, v_col, g_col):
    b, s, _ = u.shape
    n_chain = (SB_TQ // SB_SUB) * (width // LANES)
    qb, kb, vb, gb = (c // width for c in (q_col, k_col, v_col, g_col))
    return pl.pallas_call(
        _stick_break_kernel,
        out_shape=jax.ShapeDtypeStruct((b, s, width), BF16),
        grid=(b, s // SB_TQ),
        in_specs=[
            pl.BlockSpec((None, SB_TQ, width), lambda bi, i: (bi, i, qb)),
            pl.BlockSpec((None, s, width), lambda bi, i: (bi, 0, kb)),
            pl.BlockSpec((None, s, width), lambda bi, i: (bi, 0, vb)),
            pl.BlockSpec((None, SB_TQ, width), lambda bi, i: (bi, i, gb)),
            pl.BlockSpec(suffix.shape, lambda bi, i: (0, 0)),
        ],
        out_specs=pl.BlockSpec((None, SB_TQ, width), lambda bi, i: (bi, i, 0)),
        scratch_shapes=[pltpu.VMEM((n_chain, 2 * SB_SUB, LANES), F32),
                        pltpu.VMEM((n_chain, 2 * SB_SUB, LANES), F32)],
        compiler_params=pltpu.CompilerParams(
            dimension_semantics=("parallel", "arbitrary"), vmem_limit_bytes=VMEM_LIMIT),
        name="stick_break",
    )(u, u, u, u, suffix)


def _mix_out_kernel(seq_tiles, ua_ref, halo_ref, ysb_ref, m0_ref, m1_ref, m2_ref, x_ref,
                    pw_ref, ps_ref, cw_ref, cb_ref, wb_ref, wo_ref, gp_ref, o_ref):
    tm = x_ref.shape[0]
    w = ysb_ref.shape[1]
    ti = pl.program_id(0) % seq_tiles

    def ext(col):
        head = jnp.where(ti == 0, 0.0, halo_ref[:, col:col + w].astype(F32))
        return jnp.concatenate([head, ua_ref[:, col:col + w].astype(F32)], axis=0)

    def back(a, r):
        return pltpu.roll(a, r, axis=0)

    pos = ti * tm + lax.broadcasted_iota(jnp.int32, (tm, 1), 0)
    v = ext(0)
    mixed = []
    for gi, win in enumerate(POOL_WINDOWS):
        vg = v[:, gi * POOL_GROUP:(gi + 1) * POOL_GROUP]
        ssum = vg
        span = 1
        while span < win:
            ssum = ssum + back(ssum, span)
            span *= 2
        inv_cnt = 1.0 / jnp.minimum(pos + 1, win).astype(F32)
        pooled = ssum[HALO:] * inv_cnt - vg[HALO:]
        mixed.append(jnp.dot(pooled.astype(BF16), pw_ref[gi], preferred_element_type=F32))
    y_pool = jnp.concatenate(mixed, axis=1) * ps_ref[...] * _silu(ua_ref[:, w:2 * w].astype(F32))

    z = ext(4 * w) * ext(2 * w)
    y = cw_ref[0:1, :] * back(z, 2)[HALO:] + cw_ref[1:2, :] * back(z, 1)[HALO:] + cw_ref[2:3, :] * z[HALO:]
    y_conv = (ua_ref[:, 3 * w:4 * w].astype(F32) * (y + cb_ref[...])
              * _silu(ua_ref[:, 5 * w:6 * w].astype(F32)))

    merged = None
    for n, (yb, m_ref) in enumerate(((y_pool.astype(BF16), m0_ref), (y_conv.astype(BF16), m1_ref),
                                     (ysb_ref[...], m2_ref))):
        proj = jnp.dot(yb, wb_ref[n], preferred_element_type=F32)
        gated = _sigmoid(m_ref[...].astype(F32)) * proj
        merged = gated if merged is None else merged + gated

    out = jnp.dot(merged.astype(BF16), wo_ref[...], preferred_element_type=F32)
    ms = jnp.mean(out * out, axis=-1, keepdims=True)
    o_ref[...] = x_ref[...] + out * lax.rsqrt(ms + RMS_EPS) * gp_ref[...]


def _mix_out(layer, u, ysb, x, seq_len, pool_w, pool_scale, conv_w, conv_b, w_branch, w_out, g_post):
    m, d = x.shape
    w = ysb.shape[1]
    tm = MIX_TM
    halo_blocks = tm // HALO
    merge_blk = (10 * w) // d

    def param(a):
        return pl.BlockSpec((None,) + a.shape[1:], lambda i: (layer,) + (0,) * (a.ndim - 1))

    return pl.pallas_call(
        functools.partial(_mix_out_kernel, seq_len // tm),
        out_shape=jax.ShapeDtypeStruct((m, d), F32),
        grid=(m // tm,),
        in_specs=[
            pl.BlockSpec((tm, 6 * w), lambda i: (i, 0)),
            pl.BlockSpec((HALO, 6 * w), lambda i: (jnp.maximum(i * halo_blocks - 1, 0), 0)),
            pl.BlockSpec((tm, w), lambda i: (i, 0)),
            pl.BlockSpec((tm, d), lambda i: (i, merge_blk)),
            pl.BlockSpec((tm, d), lambda i: (i, merge_blk + 1)),
            pl.BlockSpec((tm, d), lambda i: (i, merge_blk + 2)),
            pl.BlockSpec((tm, d), lambda i: (i, 0)),
            param(pool_w), param(pool_scale), param(conv_w), param(conv_b),
            param(w_branch), param(w_out), param(g_post),
        ],
        out_specs=pl.BlockSpec((tm, d), lambda i: (i, 0)),
        compiler_params=pltpu.CompilerParams(
            dimension_semantics=("parallel",), vmem_limit_bytes=VMEM_LIMIT),
        name="mix_out",
    )(u, u, ysb, u, u, u, x, pool_w, pool_scale, conv_w, conv_b, w_branch, w_out, g_post)


def kernel(x, pre_norm_g, w_in, pool_w, pool_scale, conv_w, conv_b, w_branch, w_out, post_norm_g):
    b, s, d = x.shape
    depth = w_in.shape[0]
    n_in = w_in.shape[2]
    w = pool_scale.shape[1]
    assert n_in == 10 * w + 3 * d and w % LANES == 0 and (10 * w) % d == 0
    assert (b * s) % IN_TM == 0 and n_in % IN_TN == 0 and s % SB_TQ == 0 and s % MIX_TM == 0
    assert s >= SB_WIN and SB_WIN % SB_SUB == 0
    key = jnp.arange(SB_WIN)
    suffix = (key[:, None] > key[None, :]).astype(BF16)
    xf = x.reshape(b * s, d)
    g_pre = pre_norm_g.reshape(depth, 1, d)
    g_post = post_norm_g.reshape(depth, 1, d)
    pool_scale = pool_scale.reshape(depth, 1, w)
    conv_b = conv_b.reshape(depth, 1, w)
    pool_w, w_branch, w_out = (a.astype(BF16) for a in (pool_w, w_branch, w_out))
    for l in range(depth):
        u = _in_proj(l, xf, g_pre, w_in)
        ysb = _stick_break(u.reshape(b, s, n_in), suffix, w, 6 * w, 7 * w, 8 * w, 9 * w)
        xf = _mix_out(l, u, ysb.reshape(b * s, w), xf, s,
                      pool_w, pool_scale, conv_w, conv_b, w_branch, w_out, g_post)
    return xf.reshape(b, s, d)
```

```python
import functools

import jax
import jax.numpy as jnp
from jax import lax
from jax.experimental import pallas as pl
from jax.experimental.pallas import tpu as pltpu

F32 = jnp.float32
BF16 = jnp.bfloat16

LANES = 128
HALO = 16
POOL_WINDOWS = (2, 4, 8, 16)
POOL_GROUP = 128
CONV_K = 3
HEAD_DIM = 64
RMS_EPS = 1e-6
LOG2E = 1.4426950408889634

IN_TM = 2048
IN_TN = 2048
SB_TQ = 1024
SB_SUB = 64
SB_WIN = 256
MIX_TM = 512
VMEM_LIMIT = 56 * 1024 * 1024
IN_VMEM_LIMIT = 62 * 1024 * 1024

SB_DEAD_MASS = 106.0


def _sigmoid(x):
    return 1.0 / (1.0 + jnp.exp2(x * -LOG2E))


def _silu(x):
    return x * _sigmoid(x)


def _softplus(z):
    return jnp.maximum(z, jnp.log(1.0 + jnp.exp(jnp.minimum(z, 44.0))))


def _in_proj_kernel(x_ref, g_ref, w_ref, u_ref, h_ref):
    @pl.when(pl.program_id(1) == 0)
    def _():
        x = x_ref[...]
        ms = jnp.mean(x * x, axis=-1, keepdims=True)
        h_ref[...] = (x * lax.rsqrt(ms + RMS_EPS) * g_ref[...]).astype(BF16)

    u_ref[...] = jnp.dot(h_ref[...], w_ref[...].astype(BF16),
                         preferred_element_type=F32).astype(u_ref.dtype)


def _in_proj(layer, x, g, w):
    m, d = x.shape
    n = w.shape[2]
    return pl.pallas_call(
        _in_proj_kernel,
        out_shape=jax.ShapeDtypeStruct((m, n), BF16),
        grid=(m // IN_TM, n // IN_TN),
        in_specs=[
            pl.BlockSpec((IN_TM, d), lambda i, j: (i, 0)),
            pl.BlockSpec((None, 1, d), lambda i, j: (layer, 0, 0)),
            pl.BlockSpec((None, d, IN_TN), lambda i, j: (layer, 0, j)),
        ],
        out_specs=pl.BlockSpec((IN_TM, IN_TN), lambda i, j: (i, j)),
        scratch_shapes=[pltpu.VMEM((IN_TM, d), BF16)],
        compiler_params=pltpu.CompilerParams(
            dimension_semantics=("parallel", "arbitrary"), vmem_limit_bytes=IN_VMEM_LIMIT),
        name="in_proj",
    )(x, g, w)


def _stick_break_kernel(q_ref, k_ref, v_ref, g_ref, suf_ref, o_ref, acc_ref, run_ref):
    sub, win = SB_SUB, SB_WIN
    n_sub = SB_TQ // sub
    n_pair = q_ref.shape[1] // LANES
    hist = win - sub
    q0 = pl.program_id(1) * SB_TQ

    lane = lax.broadcasted_iota(jnp.int32, (sub, LANES), 1)
    first = lane < HEAD_DIM
    col = lax.broadcasted_iota(jnp.int32, (2 * sub, win), 1)
    rsub = lax.broadcasted_iota(jnp.int32, (2 * sub, win), 0) & (sub - 1)
    scale = jnp.asarray(HEAD_DIM ** -0.5, BF16)
    nt = (((1,), (1,)), ((), ()))

    def stacked_q(s, p):
        q = q_ref[s * sub:(s + 1) * sub, p * LANES:(p + 1) * LANES]
        zero = jnp.zeros_like(q)
        return jnp.concatenate([jnp.where(first, q, zero), jnp.where(first, zero, q)], axis=0) * scale

    def logits(s, p, start):
        kt = k_ref[pl.ds(start, win), p * LANES:(p + 1) * LANES]
        return lax.dot_general(stacked_q(s, p), kt, nt, preferred_element_type=F32)

    def weigh(s, p, start, mask, log_beta, later):
        vt = v_ref[pl.ds(start, win), p * LANES:(p + 1) * LANES]
        w = jnp.where(mask, jnp.exp(log_beta - later), 0.0)
        return jnp.dot(w.astype(BF16), vt, preferred_element_type=F32)

    def key_tile(s, p, start, mask, run):
        z = logits(s, p, start)
        sp = _softplus(z)
        log_beta = z - sp
        sp = jnp.where(mask, sp, 0.0)
        cum = jnp.dot(sp.astype(BF16), suf_ref[...], preferred_element_type=F32)
        return weigh(s, p, start, mask, log_beta, cum + run), cum[:, 0:1] + sp[:, 0:1]

    def emit(s, p, acc):
        rows = slice(s * sub, (s + 1) * sub)
        cols = slice(p * LANES, (p + 1) * LANES)
        out = jnp.where(first, acc[:sub, :], acc[sub:, :])
        o_ref[rows, cols] = (out * _silu(g_ref[rows, cols].astype(F32))).astype(o_ref.dtype)

    win_start = [pl.multiple_of(jnp.maximum(q0 + s * sub - hist, 0), sub) for s in range(n_sub)]
    masks = [col < rsub + (q0 + s * sub - win_start[s]) for s in range(n_sub)]
    chains = [(s, p) for s in range(n_sub) for p in range(n_pair)]
    rows = 2 * sub
    zs = [logits(s, p, win_start[s]) for s, p in chains]
    sps, log_betas = [], []
    for (s, p), z in zip(chains, zs):
        sp = _softplus(z)
        log_betas.append(z - sp)
        sps.append(jnp.where(masks[s], sp, 0.0))
    cum_all = jnp.dot(jnp.concatenate([sp.astype(BF16) for sp in sps], axis=0), suf_ref[...],
                      preferred_element_type=F32)
    accs, masses = {}, {}
    for idx, (s, p) in enumerate(chains):
        cum = cum_all[idx * rows:(idx + 1) * rows]
        accs[s, p] = weigh(s, p, win_start[s], masks[s], log_betas[idx], cum)
        masses[s, p] = cum[:, 0:1] + sps[idx][:, 0:1]
        emit(s, p, accs[s, p])

    def pending(s, mass):
        return jnp.where(win_start[s] > 0, mass, SB_DEAD_MASS)

    least = functools.reduce(jnp.minimum, [pending(s, m) for (s, _), m in masses.items()])
    older_end = win_start[n_sub - 1]

    @pl.when((older_end > 0) & (jnp.min(least) < SB_DEAD_MASS))
    def _():
        for idx, key in enumerate(accs):
            acc_ref[idx] = accs[key]
            run_ref[idx] = jnp.broadcast_to(masses[key], (2 * sub, LANES))

        def live(carry):
            end, least_mass = carry
            return (end > 0) & (least_mass < SB_DEAD_MASS)

        def older(carry):
            end, _ = carry
            start = pl.multiple_of(jnp.maximum(end - win, 0), sub)
            least_mass = None
            for idx, (s, p) in enumerate(accs):
                mask = col + start < jnp.minimum(win_start[s], end)
                run = run_ref[idx]
                pv, mass = key_tile(s, p, start, mask, jnp.concatenate([run] * (win // LANES), axis=1))
                acc_ref[idx] += pv
                run = run + jnp.broadcast_to(mass, run.shape)
                run_ref[idx] = run
                low = jnp.min(pending(s, run))
                least_mass = low if least_mass is None else jnp.minimum(least_mass, low)
            return start, least_mass

        lax.while_loop(live, older, (older_end, jnp.min(least)))
        for idx, (s, p) in enumerate(accs):
            emit(s, p, acc_ref[idx])


def _stick_break(u, suffix, width, q_col, k_col, v_col, g_col):
    b, s, _ = u.shape
    n_chain = (SB_TQ // SB_SUB) * (width // LANES)
    qb, kb, vb, gb = (c // width for c in (q_col, k_col, v_col, g_col))
    return pl.pallas_call(
        _stick_break_kernel,
        out_shape=jax.ShapeDtypeStruct((b, s, width), BF16),
        grid=(b, s // SB_TQ),
        in_specs=[
            pl.BlockSpec((None, SB_TQ, width), lambda bi, i: (bi, i, qb)),
            pl.BlockSpec((None, s, width), lambda bi, i: (bi, 0, kb)),
            pl.BlockSpec((None, s, width), lambda bi, i: (bi, 0, vb)),
            pl.BlockSpec((None, SB_TQ, width), lambda bi, i: (bi, i, gb)),
            pl.BlockSpec(suffix.shape, lambda bi, i: (0, 0)),
        ],
        out_specs=pl.BlockSpec((None, SB_TQ, width), lambda bi, i: (bi, i, 0)),
        scratch_shapes=[pltpu.VMEM((n_chain, 2 * SB_SUB, LANES), F32),
                        pltpu.VMEM((n_chain, 2 * SB_SUB, LANES), F32)],
        compiler_params=pltpu.CompilerParams(
            dimension_semantics=("parallel", "arbitrary"), vmem_limit_bytes=VMEM_LIMIT),
        name="stick_break",
    )(u, u, u, u, suffix)


def _mix_out_kernel(seq_tiles, ua_ref, halo_ref, ysb_ref, m0_ref, m1_ref, m2_ref, x_ref,
                    pw_ref, ps_ref, cw_ref, cb_ref, wb_ref, wo_ref, gp_ref, o_ref):
    tm = x_ref.shape[0]
    w = ysb_ref.shape[1]
    ti = pl.program_id(0) % seq_tiles

    def ext(col):
        head = jnp.where(ti == 0, 0.0, halo_ref[:, col:col + w].astype(F32))
        return jnp.concatenate([head, ua_ref[:, col:col + w].astype(F32)], axis=0)

    def back(a, r):
        return pltpu.roll(a, r, axis=0)

    pos = ti * tm + lax.broadcasted_iota(jnp.int32, (tm, 1), 0)
    v = ext(0)
    mixed = []
    for gi, win in enumerate(POOL_WINDOWS):
        vg = v[:, gi * POOL_GROUP:(gi + 1) * POOL_GROUP]
        ssum = vg
        span = 1
        while span < win:
            ssum = ssum + back(ssum, span)
            span *= 2
        inv_cnt = 1.0 / jnp.minimum(pos + 1, win).astype(F32)
        pooled = ssum[HALO:] * inv_cnt - vg[HALO:]
        mixed.append(jnp.dot(pooled.astype(BF16), pw_ref[gi], preferred_element_type=F32))
    y_pool = jnp.concatenate(mixed, axis=1) * ps_ref[...] * _silu(ua_ref[:, w:2 * w].astype(F32))

    z = ext(4 * w) * ext(2 * w)
    y = cw_ref[0:1, :] * back(z, 2)[HALO:] + cw_ref[1:2, :] * back(z, 1)[HALO:] + cw_ref[2:3, :] * z[HALO:]
    y_conv = (ua_ref[:, 3 * w:4 * w].astype(F32) * (y + cb_ref[...])
              * _silu(ua_ref[:, 5 * w:6 * w].astype(F32)))

    merged = None
    for n, (yb, m_ref) in enumerate(((y_pool.astype(BF16), m0_ref), (y_conv.astype(BF16), m1_ref),
                                     (ysb_ref[...], m2_ref))):
        proj = jnp.dot(yb, wb_ref[n], preferred_element_type=F32)
        gated = _sigmoid(m_ref[...].astype(F32)) * proj
        merged = gated if merged is None else merged + gated

    out = jnp.dot(merged.astype(BF16), wo_ref[...], preferred_element_type=F32)
    ms = jnp.mean(out * out, axis=-1, keepdims=True)
    o_ref[...] = x_ref[...] + out * lax.rsqrt(ms + RMS_EPS) * gp_ref[...]


def _mix_out(layer, u, ysb, x, seq_len, pool_w, pool_scale, conv_w, conv_b, w_branch, w_out, g_post):
    m, d = x.shape
    w = ysb.shape[1]
    tm = MIX_TM
    halo_blocks = tm // HALO
    merge_blk = (10 * w) // d

    def param(a):
        return pl.BlockSpec((None,) + a.shape[1:], lambda i: (layer,) + (0,) * (a.ndim - 1))

    return pl.pallas_call(
        functools.partial(_mix_out_kernel, seq_len // tm),
        out_shape=jax.ShapeDtypeStruct((m, d), F32),
        grid=(m // tm,),
        in_specs=[
            pl.BlockSpec((tm, 6 * w), lambda i: (i, 0)),
            pl.BlockSpec((HALO, 6 * w), lambda i: (jnp.maximum(i * halo_blocks - 1, 0), 0)),
            pl.BlockSpec((tm, w), lambda i: (i, 0)),
            pl.BlockSpec((tm, d), lambda i: (i, merge_blk)),
            pl.BlockSpec((tm, d), lambda i: (i, merge_blk + 1)),
            pl.BlockSpec((tm, d), lambda i: (i, merge_blk + 2)),
            pl.BlockSpec((tm, d), lambda i: (i, 0)),
            param(pool_w), param(pool_scale), param(conv_w), param(conv_b),
            param(w_branch), param(w_out), param(g_post),
        ],
        out_specs=pl.BlockSpec((tm, d), lambda i: (i, 0)),
        compiler_params=pltpu.CompilerParams(
            dimension_semantics=("parallel",), vmem_limit_bytes=VMEM_LIMIT),
        name="mix_out",
    )(u, u, ysb, u, u, u, x, pool_w, pool_scale, conv_w, conv_b, w_branch, w_out, g_post)


def kernel(x, pre_norm_g, w_in, pool_w, pool_scale, conv_w, conv_b, w_branch, w_out, post_norm_g):
    b, s, d = x.shape
    depth = w_in.shape[0]
    n_in = w_in.shape[2]
    w = pool_scale.shape[1]
    assert n_in == 10 * w + 3 * d and w % LANES == 0 and (10 * w) % d == 0
    assert (b * s) % IN_TM == 0 and n_in % IN_TN == 0 and s % SB_TQ == 0 and s % MIX_TM == 0
    assert s >= SB_WIN and SB_WIN % SB_SUB == 0
    key = jnp.arange(SB_WIN)
    suffix = (key[:, None] > key[None, :]).astype(BF16)
    xf = x.reshape(b * s, d)
    g_pre = pre_norm_g.reshape(depth, 1, d)
    g_post = post_norm_g.reshape(depth, 1, d)
    pool_scale = pool_scale.reshape(depth, 1, w)
    conv_b = conv_b.reshape(depth, 1, w)
    pool_w, w_branch, w_out = (a.astype(BF16) for a in (pool_w, w_branch, w_out))
    for l in range(depth):
        u = _in_proj(l, xf, g_pre, w_in)
        ysb = _stick_break(u.reshape(b, s, n_in), suffix, w, 6 * w, 7 * w, 8 * w, 9 * w)
        xf = _mix_out(l, u, ysb.reshape(b * s, w), xf, s,
                      pool_w, pool_scale, conv_w, conv_b, w_branch, w_out, g_post)
    return xf.reshape(b, s, d)
```

```python
import functools

import jax
import jax.numpy as jnp
from jax import lax
from jax.experimental import pallas as pl
from jax.experimental.pallas import tpu as pltpu

F32 = jnp.float32
BF16 = jnp.bfloat16

LANES = 128
HALO = 16
POOL_WINDOWS = (2, 4, 8, 16)
POOL_GROUP = 128
CONV_K = 3
HEAD_DIM = 64
RMS_EPS = 1e-6
LOG2E = 1.4426950408889634

IN_TM = 2048
IN_TN = 2048
SB_TQ = 1024
SB_SUB = 64
SB_WIN = 256
MIX_TM = 512
VMEM_LIMIT = 56 * 1024 * 1024
IN_VMEM_LIMIT = 60 * 1024 * 1024

SB_DEAD_MASS = 106.0


def _sigmoid(x):
    return 1.0 / (1.0 + jnp.exp2(x * -LOG2E))


def _silu(x):
    return x * _sigmoid(x)


def _softplus(z):
    return jnp.maximum(z, jnp.log(1.0 + jnp.exp(jnp.minimum(z, 44.0))))


def _in_proj_kernel(x_ref, g_ref, w_ref, u_ref, h_ref):
    @pl.when(pl.program_id(1) == 0)
    def _():
        x = x_ref[...]
        ms = jnp.mean(x * x, axis=-1, keepdims=True)
        h_ref[...] = (x * lax.rsqrt(ms + RMS_EPS) * g_ref[...]).astype(BF16)

    u_ref[...] = jnp.dot(h_ref[...], w_ref[...].astype(BF16),
                         preferred_element_type=F32).astype(u_ref.dtype)


def _in_proj(layer, x, g, w):
    m, d = x.shape
    n = w.shape[2]
    return pl.pallas_call(
        _in_proj_kernel,
        out_shape=jax.ShapeDtypeStruct((m, n), BF16),
        grid=(m // IN_TM, n // IN_TN),
        in_specs=[
            pl.BlockSpec((IN_TM, d), lambda i, j: (i, 0)),
            pl.BlockSpec((None, 1, d), lambda i, j: (layer, 0, 0)),
            pl.BlockSpec((None, d, IN_TN), lambda i, j: (layer, 0, j)),
        ],
        out_specs=pl.BlockSpec((IN_TM, IN_TN), lambda i, j: (i, j)),
        scratch_shapes=[pltpu.VMEM((IN_TM, d), BF16)],
        compiler_params=pltpu.CompilerParams(
            dimension_semantics=("parallel", "arbitrary"), vmem_limit_bytes=IN_VMEM_LIMIT),
        name="in_proj",
    )(x, g, w)


def _stick_break_kernel(q_ref, k_ref, v_ref, g_ref, suf_ref, o_ref, acc_ref, run_ref):
    sub, win = SB_SUB, SB_WIN
    n_sub = SB_TQ // sub
    n_pair = q_ref.shape[1] // LANES
    hist = win - sub
    q0 = pl.program_id(1) * SB_TQ

    lane = lax.broadcasted_iota(jnp.int32, (sub, LANES), 1)
    first = lane < HEAD_DIM
    col = lax.broadcasted_iota(jnp.int32, (2 * sub, win), 1)
    rsub = lax.broadcasted_iota(jnp.int32, (2 * sub, win), 0) & (sub - 1)
    scale = jnp.asarray(HEAD_DIM ** -0.5, BF16)
    nt = (((1,), (1,)), ((), ()))

    def stacked_q(s, p):
        q = q_ref[s * sub:(s + 1) * sub, p * LANES:(p + 1) * LANES]
        zero = jnp.zeros_like(q)
        return jnp.concatenate([jnp.where(first, q, zero), jnp.where(first, zero, q)], axis=0) * scale

    def logits(s, p, start):
        kt = k_ref[pl.ds(start, win), p * LANES:(p + 1) * LANES]
        return lax.dot_general(stacked_q(s, p), kt, nt, preferred_element_type=F32)

    def weigh(s, p, start, mask, log_beta, later):
        vt = v_ref[pl.ds(start, win), p * LANES:(p + 1) * LANES]
        w = jnp.where(mask, jnp.exp(log_beta - later), 0.0)
        return jnp.dot(w.astype(BF16), vt, preferred_element_type=F32)

    def key_tile(s, p, start, mask, run):
        z = logits(s, p, start)
        sp = _softplus(z)
        log_beta = z - sp
        sp = jnp.where(mask, sp, 0.0)
        cum = jnp.dot(sp.astype(BF16), suf_ref[...], preferred_element_type=F32)
        return weigh(s, p, start, mask, log_beta, cum + run), cum[:, 0:1] + sp[:, 0:1]

    def emit(s, p, acc):
        rows = slice(s * sub, (s + 1) * sub)
        cols = slice(p * LANES, (p + 1) * LANES)
        out = jnp.where(first, acc[:sub, :], acc[sub:, :])
        o_ref[rows, cols] = (out * _silu(g_ref[rows, cols].astype(F32))).astype(o_ref.dtype)

    win_start = [pl.multiple_of(jnp.maximum(q0 + s * sub - hist, 0), sub) for s in range(n_sub)]
    masks = [col < rsub + (q0 + s * sub - win_start[s]) for s in range(n_sub)]
    chains = [(s, p) for s in range(n_sub) for p in range(n_pair)]
    rows = 2 * sub
    zs = [logits(s, p, win_start[s]) for s, p in chains]
    sps, log_betas = [], []
    for (s, p), z in zip(chains, zs):
        sp = _softplus(z)
        log_betas.append(z - sp)
        sps.append(jnp.where(masks[s], sp, 0.0))
    cum_all = jnp.dot(jnp.concatenate([sp.astype(BF16) for sp in sps], axis=0), suf_ref[...],
                      preferred_element_type=F32)
    accs, masses = {}, {}
    for idx, (s, p) in enumerate(chains):
        cum = cum_all[idx * rows:(idx + 1) * rows]
        accs[s, p] = weigh(s, p, win_start[s], masks[s], log_betas[idx], cum)
        masses[s, p] = cum[:, 0:1] + sps[idx][:, 0:1]
        emit(s, p, accs[s, p])

    def pending(s, mass):
        return jnp.where(win_start[s] > 0, mass, SB_DEAD_MASS)

    least = functools.reduce(jnp.minimum, [pending(s, m) for (s, _), m in masses.items()])
    older_end = win_start[n_sub - 1]

    @pl.when((older_end > 0) & (jnp.min(least) < SB_DEAD_MASS))
    def _():
        for idx, key in enumerate(accs):
            acc_ref[idx] = accs[key]
            run_ref[idx] = jnp.broadcast_to(masses[key], (2 * sub, LANES))

        def live(carry):
            end, least_mass = carry
            return (end > 0) & (least_mass < SB_DEAD_MASS)

        def older(carry):
            end, _ = carry
            start = pl.multiple_of(jnp.maximum(end - win, 0), sub)
            least_mass = None
            for idx, (s, p) in enumerate(accs):
                mask = col + start < jnp.minimum(win_start[s], end)
                run = run_ref[idx]
                pv, mass = key_tile(s, p, start, mask, jnp.concatenate([run] * (win // LANES), axis=1))
                acc_ref[idx] += pv
                run = run + jnp.broadcast_to(mass, run.shape)
                run_ref[idx] = run
                low = jnp.min(pending(s, run))
                least_mass = low if least_mass is None else jnp.minimum(least_mass, low)
            return start, least_mass

        lax.while_loop(live, older, (older_end, jnp.min(least)))
        for idx, (s, p) in enumerate(accs):
            emit(s, p, acc_ref[idx])


def _stick_break(u, suffix, width, q_col, k_col, v_col, g_col):
    b, s, _ = u.shape
    n_chain = (SB_TQ // SB_SUB) * (width // LANES)
    qb, kb, vb, gb = (c // width for c in (q_col, k_col, v_col, g_col))
    return pl.pallas_call(
        _stick_break_kernel,
        out_shape=jax.ShapeDtypeStruct((b, s, width), BF16),
        grid=(b, s // SB_TQ),
        in_specs=[
            pl.BlockSpec((None, SB_TQ, width), lambda bi, i: (bi, i, qb)),
            pl.BlockSpec((None, s, width), lambda bi, i: (bi, 0, kb)),
            pl.BlockSpec((None, s, width), lambda bi, i: (bi, 0, vb)),
            pl.BlockSpec((None, SB_TQ, width), lambda bi, i: (bi, i, gb)),
            pl.BlockSpec(suffix.shape, lambda bi, i: (0, 0)),
        ],
        out_specs=pl.BlockSpec((None, SB_TQ, width), lambda bi, i: (bi, i, 0)),
        scratch_shapes=[pltpu.VMEM((n_chain, 2 * SB_SUB, LANES), F32),
                        pltpu.VMEM((n_chain, 2 * SB_SUB, LANES), F32)],
        compiler_params=pltpu.CompilerParams(
            dimension_semantics=("parallel", "arbitrary"), vmem_limit_bytes=VMEM_LIMIT),
        name="stick_break",
    )(u, u, u, u, suffix)


def _mix_out_kernel(seq_tiles, ua_ref, halo_ref, ysb_ref, m0_ref, m1_ref, m2_ref, x_ref,
                    pw_ref, ps_ref, cw_ref, cb_ref, wb_ref, wo_ref, gp_ref, o_ref):
    tm = x_ref.shape[0]
    w = ysb_ref.shape[1]
    ti = pl.program_id(0) % seq_tiles

    def ext(col):
        head = jnp.where(ti == 0, 0.0, halo_ref[:, col:col + w].astype(F32))
        return jnp.concatenate([head, ua_ref[:, col:col + w].astype(F32)], axis=0)

    def back(a, r):
        return pltpu.roll(a, r, axis=0)

    pos = ti * tm + lax.broadcasted_iota(jnp.int32, (tm, 1), 0)
    v = ext(0)
    mixed = []
    for gi, win in enumerate(POOL_WINDOWS):
        vg = v[:, gi * POOL_GROUP:(gi + 1) * POOL_GROUP]
        ssum = vg
        span = 1
        while span < win:
            ssum = ssum + back(ssum, span)
            span *= 2
        inv_cnt = 1.0 / jnp.minimum(pos + 1, win).astype(F32)
        pooled = ssum[HALO:] * inv_cnt - vg[HALO:]
        mixed.append(jnp.dot(pooled.astype(BF16), pw_ref[gi], preferred_element_type=F32))
    y_pool = jnp.concatenate(mixed, axis=1) * ps_ref[...] * _silu(ua_ref[:, w:2 * w].astype(F32))

    z = ext(4 * w) * ext(2 * w)
    y = cw_ref[0:1, :] * back(z, 2)[HALO:] + cw_ref[1:2, :] * back(z, 1)[HALO:] + cw_ref[2:3, :] * z[HALO:]
    y_conv = (ua_ref[:, 3 * w:4 * w].astype(F32) * (y + cb_ref[...])
              * _silu(ua_ref[:, 5 * w:6 * w].astype(F32)))

    merged = None
    for n, (yb, m_ref) in enumerate(((y_pool.astype(BF16), m0_ref), (y_conv.astype(BF16), m1_ref),
                                     (ysb_ref[...], m2_ref))):
        proj = jnp.dot(yb, wb_ref[n], preferred_element_type=F32)
        gated = _sigmoid(m_ref[...].astype(F32)) * proj
        merged = gated if merged is None else merged + gated

    out = jnp.dot(merged.astype(BF16), wo_ref[...], preferred_element_type=F32)
    ms = jnp.mean(out * out, axis=-1, keepdims=True)
    o_ref[...] = x_ref[...] + out * lax.rsqrt(ms + RMS_EPS) * gp_ref[...]


def _mix_out(layer, u, ysb, x, seq_len, pool_w, pool_scale, conv_w, conv_b, w_branch, w_out, g_post):
    m, d = x.shape
    w = ysb.shape[1]
    tm = MIX_TM
    halo_blocks = tm // HALO
    merge_blk = (10 * w) // d

    def param(a):
        return pl.BlockSpec((None,) + a.shape[1:], lambda i: (layer,) + (0,) * (a.ndim - 1))

    return pl.pallas_call(
        functools.partial(_mix_out_kernel, seq_len // tm),
        out_shape=jax.ShapeDtypeStruct((m, d), F32),
        grid=(m // tm,),
        in_specs=[
            pl.BlockSpec((tm, 6 * w), lambda i: (i, 0)),
            pl.BlockSpec((HALO, 6 * w), lambda i: (jnp.maximum(i * halo_blocks - 1, 0), 0)),
            pl.BlockSpec((tm, w), lambda i: (i, 0)),
            pl.BlockSpec((tm, d), lambda i: (i, merge_blk)),
            pl.BlockSpec((tm, d), lambda i: (i, merge_blk + 1)),
            pl.BlockSpec((tm, d), lambda i: (i, merge_blk + 2)),
            pl.BlockSpec((tm, d), lambda i: (i, 0)),
            param(pool_w), param(pool_scale), param(conv_w), param(conv_b),
            param(w_branch), param(w_out), param(g_post),
        ],
        out_specs=pl.BlockSpec((tm, d), lambda i: (i, 0)),
        compiler_params=pltpu.CompilerParams(
            dimension_semantics=("parallel",), vmem_limit_bytes=VMEM_LIMIT),
        name="mix_out",
    )(u, u, ysb, u, u, u, x, pool_w, pool_scale, conv_w, conv_b, w_branch, w_out, g_post)


def kernel(x, pre_norm_g, w_in, pool_w, pool_scale, conv_w, conv_b, w_branch, w_out, post_norm_g):
    b, s, d = x.shape
    depth = w_in.shape[0]
    n_in = w_in.shape[2]
    w = pool_scale.shape[1]
    assert n_in == 10 * w + 3 * d and w % LANES == 0 and (10 * w) % d == 0
    assert (b * s) % IN_TM == 0 and n_in % IN_TN == 0 and s % SB_TQ == 0 and s % MIX_TM == 0
    assert s >= SB_WIN and SB_WIN % SB_SUB == 0
    key = jnp.arange(SB_WIN)
    suffix = (key[:, None] > key[None, :]).astype(BF16)
    xf = x.reshape(b * s, d)
    g_pre = pre_norm_g.reshape(depth, 1, d)
    g_post = post_norm_g.reshape(depth, 1, d)
    pool_scale = pool_scale.reshape(depth, 1, w)
    conv_b = conv_b.reshape(depth, 1, w)
    pool_w, w_branch, w_out = (a.astype(BF16) for a in (pool_w, w_branch, w_out))
    for l in range(depth):
        u = _in_proj(l, xf, g_pre, w_in)
        ysb = _stick_break(u.reshape(b, s, n_in), suffix, w, 6 * w, 7 * w, 8 * w, 9 * w)
        xf = _mix_out(l, u, ysb.reshape(b * s, w), xf, s,
                      pool_w, pool_scale, conv_w, conv_b, w_branch, w_out, g_post)
    return xf.reshape(b, s, d)
```

```python
import functools

import jax
import jax.numpy as jnp
from jax import lax
from jax.experimental import pallas as pl
from jax.experimental.pallas import tpu as pltpu

F32 = jnp.float32
BF16 = jnp.bfloat16

LANES = 128
HALO = 16
POOL_WINDOWS = (2, 4, 8, 16)
POOL_GROUP = 128
CONV_K = 3
HEAD_DIM = 64
RMS_EPS = 1e-6
LOG2E = 1.4426950408889634

IN_TM = 2048
IN_TN = 2048
SB_TQ = 1024
SB_SUB = 64
SB_WIN = 256
MIX_TM = 512
VMEM_LIMIT = 56 * 1024 * 1024
IN_VMEM_LIMIT = 60 * 1024 * 1024

SB_DEAD_MASS = 106.0


def _sigmoid(x):
    return 1.0 / (1.0 + jnp.exp2(x * -LOG2E))


def _silu(x):
    return x * _sigmoid(x)


def _softplus(z):
    return jnp.maximum(z, jnp.log(1.0 + jnp.exp(jnp.minimum(z, 44.0))))


def _in_proj_kernel(layer, x_ref, g_ref, w_ref, u_ref, h_ref):
    @pl.when(pl.program_id(1) == 0)
    def _():
        x = x_ref[...]
        ms = jnp.mean(x * x, axis=-1, keepdims=True)
        h_ref[...] = (x * lax.rsqrt(ms + RMS_EPS) * g_ref[layer:layer + 1, :]).astype(BF16)

    u_ref[...] = jnp.dot(h_ref[...], w_ref[...].astype(BF16),
                         preferred_element_type=F32).astype(u_ref.dtype)


def _in_proj(layer, x, g, w):
    m, d = x.shape
    n = w.shape[2]
    return pl.pallas_call(
        functools.partial(_in_proj_kernel, layer),
        out_shape=jax.ShapeDtypeStruct((m, n), BF16),
        grid=(m // IN_TM, n // IN_TN),
        in_specs=[
            pl.BlockSpec((IN_TM, d), lambda i, j: (i, 0)),
            pl.BlockSpec(g.shape, lambda i, j: (0, 0)),
            pl.BlockSpec((None, d, IN_TN), lambda i, j: (layer, 0, j)),
        ],
        out_specs=pl.BlockSpec((IN_TM, IN_TN), lambda i, j: (i, j)),
        scratch_shapes=[pltpu.VMEM((IN_TM, d), BF16)],
        compiler_params=pltpu.CompilerParams(
            dimension_semantics=("parallel", "arbitrary"), vmem_limit_bytes=IN_VMEM_LIMIT),
        name="in_proj",
    )(x, g, w)


def _stick_break_kernel(q_ref, k_ref, v_ref, g_ref, suf_ref, o_ref, acc_ref, run_ref):
    sub, win = SB_SUB, SB_WIN
    n_sub = SB_TQ // sub
    n_pair = q_ref.shape[1] // LANES
    hist = win - sub
    q0 = pl.program_id(1) * SB_TQ

    lane = lax.broadcasted_iota(jnp.int32, (sub, LANES), 1)
    first = lane < HEAD_DIM
    col = lax.broadcasted_iota(jnp.int32, (2 * sub, win), 1)
    rsub = lax.broadcasted_iota(jnp.int32, (2 * sub, win), 0) & (sub - 1)
    scale = jnp.asarray(HEAD_DIM ** -0.5, BF16)
    nt = (((1,), (1,)), ((), ()))

    def stacked_q(s, p):
        q = q_ref[s * sub:(s + 1) * sub, p * LANES:(p + 1) * LANES]
        zero = jnp.zeros_like(q)
        return jnp.concatenate([jnp.where(first, q, zero), jnp.where(first, zero, q)], axis=0) * scale

    def logits(s, p, start):
        kt = k_ref[pl.ds(start, win), p * LANES:(p + 1) * LANES]
        return lax.dot_general(stacked_q(s, p), kt, nt, preferred_element_type=F32)

    def weigh(s, p, start, mask, log_beta, later):
        vt = v_ref[pl.ds(start, win), p * LANES:(p + 1) * LANES]
        w = jnp.where(mask, jnp.exp(log_beta - later), 0.0)
        return jnp.dot(w.astype(BF16), vt, preferred_element_type=F32)

    def key_tile(s, p, start, mask, run):
        z = logits(s, p, start)
        sp = _softplus(z)
        log_beta = z - sp
        sp = jnp.where(mask, sp, 0.0)
        cum = jnp.dot(sp.astype(BF16), suf_ref[...], preferred_element_type=F32)
        return weigh(s, p, start, mask, log_beta, cum + run), cum[:, 0:1] + sp[:, 0:1]

    def emit(s, p, acc):
        rows = slice(s * sub, (s + 1) * sub)
        cols = slice(p * LANES, (p + 1) * LANES)
        out = jnp.where(first, acc[:sub, :], acc[sub:, :])
        o_ref[rows, cols] = (out * _silu(g_ref[rows, cols].astype(F32))).astype(o_ref.dtype)

    win_start = [pl.multiple_of(jnp.maximum(q0 + s * sub - hist, 0), sub) for s in range(n_sub)]
    masks = [col < rsub + (q0 + s * sub - win_start[s]) for s in range(n_sub)]
    chains = [(s, p) for s in range(n_sub) for p in range(n_pair)]
    rows = 2 * sub
    zs = [logits(s, p, win_start[s]) for s, p in chains]
    sps, log_betas = [], []
    for (s, p), z in zip(chains, zs):
        sp = _softplus(z)
        log_betas.append(z - sp)
        sps.append(jnp.where(masks[s], sp, 0.0))
    cum_all = jnp.dot(jnp.concatenate([sp.astype(BF16) for sp in sps], axis=0), suf_ref[...],
                      preferred_element_type=F32)
    accs, masses = {}, {}
    for idx, (s, p) in enumerate(chains):
        cum = cum_all[idx * rows:(idx + 1) * rows]
        accs[s, p] = weigh(s, p, win_start[s], masks[s], log_betas[idx], cum)
        masses[s, p] = cum[:, 0:1] + sps[idx][:, 0:1]
        emit(s, p, accs[s, p])

    def pending(s, mass):
        return jnp.where(win_start[s] > 0, mass, SB_DEAD_MASS)

    least = functools.reduce(jnp.minimum, [pending(s, m) for (s, _), m in masses.items()])
    older_end = win_start[n_sub - 1]

    @pl.when((older_end > 0) & (jnp.min(least) < SB_DEAD_MASS))
    def _():
        for idx, key in enumerate(accs):
            acc_ref[idx] = accs[key]
            run_ref[idx] = jnp.broadcast_to(masses[key], (2 * sub, LANES))

        def live(carry):
            end, least_mass = carry
            return (end > 0) & (least_mass < SB_DEAD_MASS)

        def older(carry):
            end, _ = carry
            start = pl.multiple_of(jnp.maximum(end - win, 0), sub)
            least_mass = None
            for idx, (s, p) in enumerate(accs):
                mask = col + start < jnp.minimum(win_start[s], end)
                run = run_ref[idx]
                pv, mass = key_tile(s, p, start, mask, jnp.concatenate([run] * (win // LANES), axis=1))
                acc_ref[idx] += pv
                run = run + jnp.broadcast_to(mass, run.shape)
                run_ref[idx] = run
                low = jnp.min(pending(s, run))
                least_mass = low if least_mass is None else jnp.minimum(least_mass, low)
            return start, least_mass

        lax.while_loop(live, older, (older_end, jnp.min(least)))
        for idx, (s, p) in enumerate(accs):
            emit(s, p, acc_ref[idx])


def _stick_break(u, suffix, width, q_col, k_col, v_col, g_col):
    b, s, _ = u.shape
    n_chain = (SB_TQ // SB_SUB) * (width // LANES)
    qb, kb, vb, gb = (c // width for c in (q_col, k_col, v_col, g_col))
    return pl.pallas_call(
        _stick_break_kernel,
        out_shape=jax.ShapeDtypeStruct((b, s, width), BF16),
        grid=(b, s // SB_TQ),
        in_specs=[
            pl.BlockSpec((None, SB_TQ, width), lambda bi, i: (bi, i, qb)),
            pl.BlockSpec((None, s, width), lambda bi, i: (bi, 0, kb)),
            pl.BlockSpec((None, s, width), lambda bi, i: (bi, 0, vb)),
            pl.BlockSpec((None, SB_TQ, width), lambda bi, i: (bi, i, gb)),
            pl.BlockSpec(suffix.shape, lambda bi, i: (0, 0)),
        ],
        out_specs=pl.BlockSpec((None, SB_TQ, width), lambda bi, i: (bi, i, 0)),
        scratch_shapes=[pltpu.VMEM((n_chain, 2 * SB_SUB, LANES), F32),
                        pltpu.VMEM((n_chain, 2 * SB_SUB, LANES), F32)],
        compiler_params=pltpu.CompilerParams(
            dimension_semantics=("parallel", "arbitrary"), vmem_limit_bytes=VMEM_LIMIT),
        name="stick_break",
    )(u, u, u, u, suffix)


def _mix_out_kernel(layer, seq_tiles, ua_ref, halo_ref, ysb_ref, m0_ref, m1_ref, m2_ref, x_ref,
                    pw_ref, ps_ref, cw_ref, cb_ref, wb_ref, wo_ref, gp_ref, o_ref):
    tm = x_ref.shape[0]
    pool_scale, conv_bias, g_post = (r[layer:layer + 1, :] for r in (ps_ref, cb_ref, gp_ref))
    w = ysb_ref.shape[1]
    ti = pl.program_id(0) % seq_tiles

    def ext(col):
        head = jnp.where(ti == 0, 0.0, halo_ref[:, col:col + w].astype(F32))
        return jnp.concatenate([head, ua_ref[:, col:col + w].astype(F32)], axis=0)

    def back(a, r):
        return pltpu.roll(a, r, axis=0)

    pos = ti * tm + lax.broadcasted_iota(jnp.int32, (tm, 1), 0)
    v = ext(0)
    mixed = []
    for gi, win in enumerate(POOL_WINDOWS):
        vg = v[:, gi * POOL_GROUP:(gi + 1) * POOL_GROUP]
        ssum = vg
        span = 1
        while span < win:
            ssum = ssum + back(ssum, span)
            span *= 2
        inv_cnt = 1.0 / jnp.minimum(pos + 1, win).astype(F32)
        pooled = ssum[HALO:] * inv_cnt - vg[HALO:]
        mixed.append(jnp.dot(pooled.astype(BF16), pw_ref[gi], preferred_element_type=F32))
    y_pool = jnp.concatenate(mixed, axis=1) * pool_scale * _silu(ua_ref[:, w:2 * w].astype(F32))

    z = ext(4 * w) * ext(2 * w)
    y = cw_ref[0:1, :] * back(z, 2)[HALO:] + cw_ref[1:2, :] * back(z, 1)[HALO:] + cw_ref[2:3, :] * z[HALO:]
    y_conv = (ua_ref[:, 3 * w:4 * w].astype(F32) * (y + conv_bias)
              * _silu(ua_ref[:, 5 * w:6 * w].astype(F32)))

    merged = None
    for n, (yb, m_ref) in enumerate(((y_pool.astype(BF16), m0_ref), (y_conv.astype(BF16), m1_ref),
                                     (ysb_ref[...], m2_ref))):
        proj = jnp.dot(yb, wb_ref[n], preferred_element_type=F32)
        gated = _sigmoid(m_ref[...].astype(F32)) * proj
        merged = gated if merged is None else merged + gated

    out = jnp.dot(merged.astype(BF16), wo_ref[...], preferred_element_type=F32)
    ms = jnp.mean(out * out, axis=-1, keepdims=True)
    o_ref[...] = x_ref[...] + out * lax.rsqrt(ms + RMS_EPS) * g_post


def _mix_out(layer, u, ysb, x, seq_len, pool_w, pool_scale, conv_w, conv_b, w_branch, w_out, g_post):
    m, d = x.shape
    w = ysb.shape[1]
    tm = MIX_TM
    halo_blocks = tm // HALO
    merge_blk = (10 * w) // d

    def param(a):
        if a.ndim == 2:
            return pl.BlockSpec(a.shape, lambda i: (0, 0))
        return pl.BlockSpec((None,) + a.shape[1:], lambda i: (layer,) + (0,) * (a.ndim - 1))

    return pl.pallas_call(
        functools.partial(_mix_out_kernel, layer, seq_len // tm),
        out_shape=jax.ShapeDtypeStruct((m, d), F32),
        grid=(m // tm,),
        in_specs=[
            pl.BlockSpec((tm, 6 * w), lambda i: (i, 0)),
            pl.BlockSpec((HALO, 6 * w), lambda i: (jnp.maximum(i * halo_blocks - 1, 0), 0)),
            pl.BlockSpec((tm, w), lambda i: (i, 0)),
            pl.BlockSpec((tm, d), lambda i: (i, merge_blk)),
            pl.BlockSpec((tm, d), lambda i: (i, merge_blk + 1)),
            pl.BlockSpec((tm, d), lambda i: (i, merge_blk + 2)),
            pl.BlockSpec((tm, d), lambda i: (i, 0)),
            param(pool_w), param(pool_scale), param(conv_w), param(conv_b),
            param(w_branch), param(w_out), param(g_post),
        ],
        out_specs=pl.BlockSpec((tm, d), lambda i: (i, 0)),
        compiler_params=pltpu.CompilerParams(
            dimension_semantics=("parallel",), vmem_limit_bytes=VMEM_LIMIT),
        name="mix_out",
    )(u, u, ysb, u, u, u, x, pool_w, pool_scale, conv_w, conv_b, w_branch, w_out, g_post)


def kernel(x, pre_norm_g, w_in, pool_w, pool_scale, conv_w, conv_b, w_branch, w_out, post_norm_g):
    b, s, d = x.shape
    depth = w_in.shape[0]
    n_in = w_in.shape[2]
    w = pool_scale.shape[1]
    assert n_in == 10 * w + 3 * d and w % LANES == 0 and (10 * w) % d == 0
    assert (b * s) % IN_TM == 0 and n_in % IN_TN == 0 and s % SB_TQ == 0 and s % MIX_TM == 0
    assert s >= SB_WIN and SB_WIN % SB_SUB == 0
    key = jnp.arange(SB_WIN)
    suffix = (key[:, None] > key[None, :]).astype(BF16)
    xf = x.reshape(b * s, d)
    pool_w, w_branch, w_out = (a.astype(BF16) for a in (pool_w, w_branch, w_out))
    for l in range(depth):
        u = _in_proj(l, xf, pre_norm_g, w_in)
        ysb = _stick_break(u.reshape(b, s, n_in), suffix, w, 6 * w, 7 * w, 8 * w, 9 * w)
        xf = _mix_out(l, u, ysb.reshape(b * s, w), xf, s,
                      pool_w, pool_scale, conv_w, conv_b, w_branch, w_out, post_norm_g)
    return xf.reshape(b, s, d)
```

```python
import functools

import jax
import jax.numpy as jnp
from jax import lax
from jax.experimental import pallas as pl
from jax.experimental.pallas import tpu as pltpu

F32 = jnp.float32
BF16 = jnp.bfloat16

LANES = 128
HALO = 16
POOL_WINDOWS = (2, 4, 8, 16)
POOL_GROUP = 128
CONV_K = 3
HEAD_DIM = 64
RMS_EPS = 1e-6
LOG2E = 1.4426950408889634

IN_TM = 2048
IN_TN = 2048
SB_TQ = 1024
SB_SUB = 64
SB_WIN = 256
MIX_TM = 512
VMEM_LIMIT = 56 * 1024 * 1024
IN_VMEM_LIMIT = 60 * 1024 * 1024

SB_DEAD_MASS = 106.0


def _sigmoid(x):
    return 1.0 / (1.0 + jnp.exp2(x * -LOG2E))


def _silu(x):
    return x * _sigmoid(x)


def _softplus(z):
    return jnp.maximum(z, jnp.log(1.0 + jnp.exp(jnp.minimum(z, 44.0))))


def _in_proj_kernel(layer, x_ref, g_ref, w_ref, u_ref, h_ref):
    @pl.when(pl.program_id(1) == 0)
    def _():
        x = x_ref[...]
        ms = jnp.mean(x * x, axis=-1, keepdims=True)
        h_ref[...] = (x * lax.rsqrt(ms + RMS_EPS) * g_ref[layer:layer + 1, :]).astype(BF16)

    u_ref[...] = jnp.dot(h_ref[...], w_ref[...].astype(BF16),
                         preferred_element_type=F32).astype(u_ref.dtype)


def _in_proj(layer, x, g, w):
    m, d = x.shape
    n = w.shape[2]
    return pl.pallas_call(
        functools.partial(_in_proj_kernel, layer),
        out_shape=jax.ShapeDtypeStruct((m, n), BF16),
        grid=(m // IN_TM, n // IN_TN),
        in_specs=[
            pl.BlockSpec((IN_TM, d), lambda i, j: (i, 0)),
            pl.BlockSpec(g.shape, lambda i, j: (0, 0)),
            pl.BlockSpec((None, d, IN_TN), lambda i, j: (layer, 0, j)),
        ],
        out_specs=pl.BlockSpec((IN_TM, IN_TN), lambda i, j: (i, j)),
        scratch_shapes=[pltpu.VMEM((IN_TM, d), BF16)],
        compiler_params=pltpu.CompilerParams(
            dimension_semantics=("parallel", "arbitrary"), vmem_limit_bytes=IN_VMEM_LIMIT),
        name="in_proj",
    )(x, g, w)


def _stick_break_kernel(q_ref, k_ref, v_ref, g_ref, suf_ref, o_ref, acc_ref, run_ref):
    sub, win = SB_SUB, SB_WIN
    n_sub = SB_TQ // sub
    n_pair = q_ref.shape[1] // LANES
    hist = win - sub
    q0 = pl.program_id(1) * SB_TQ

    lane = lax.broadcasted_iota(jnp.int32, (sub, LANES), 1)
    first = lane < HEAD_DIM
    col = lax.broadcasted_iota(jnp.int32, (2 * sub, win), 1)
    rsub = lax.broadcasted_iota(jnp.int32, (2 * sub, win), 0) & (sub - 1)
    scale = jnp.asarray(HEAD_DIM ** -0.5, BF16)
    nt = (((1,), (1,)), ((), ()))

    def stacked_q(s, p):
        q = q_ref[s * sub:(s + 1) * sub, p * LANES:(p + 1) * LANES]
        zero = jnp.zeros_like(q)
        return jnp.concatenate([jnp.where(first, q, zero), jnp.where(first, zero, q)], axis=0) * scale

    def logits(s, p, start):
        kt = k_ref[pl.ds(start, win), p * LANES:(p + 1) * LANES]
        return lax.dot_general(stacked_q(s, p), kt, nt, preferred_element_type=F32)

    def weigh(s, p, start, mask, log_beta, later):
        vt = v_ref[pl.ds(start, win), p * LANES:(p + 1) * LANES]
        w = jnp.where(mask, jnp.exp(log_beta - later), 0.0)
        return jnp.dot(w.astype(BF16), vt, preferred_element_type=F32)

    def key_tile(s, p, start, mask, run):
        z = logits(s, p, start)
        sp = _softplus(z)
        log_beta = z - sp
        sp = jnp.where(mask, sp, 0.0)
        cum = jnp.dot(sp.astype(BF16), suf_ref[...], preferred_element_type=F32)
        return weigh(s, p, start, mask, log_beta, cum + run), cum[:, 0:1] + sp[:, 0:1]

    def emit(s, p, acc):
        rows = slice(s * sub, (s + 1) * sub)
        cols = slice(p * LANES, (p + 1) * LANES)
        out = jnp.where(first, acc[:sub, :], acc[sub:, :])
        o_ref[rows, cols] = (out * _silu(g_ref[rows, cols].astype(F32))).astype(o_ref.dtype)

    win_start = [pl.multiple_of(jnp.maximum(q0 + s * sub - hist, 0), sub) for s in range(n_sub)]
    masks = [col < rsub + (q0 + s * sub - win_start[s]) for s in range(n_sub)]
    chains = [(s, p) for s in range(n_sub) for p in range(n_pair)]
    rows = 2 * sub
    zs = [logits(s, p, win_start[s]) for s, p in chains]
    sps, log_betas = [], []
    for (s, p), z in zip(chains, zs):
        sp = _softplus(z)
        log_betas.append(z - sp)
        sps.append(jnp.where(masks[s], sp, 0.0))
    cum_all = jnp.dot(jnp.concatenate([sp.astype(BF16) for sp in sps], axis=0), suf_ref[...],
                      preferred_element_type=F32)
    accs, masses = {}, {}
    for idx, (s, p) in enumerate(chains):
        cum = cum_all[idx * rows:(idx + 1) * rows]
        accs[s, p] = weigh(s, p, win_start[s], masks[s], log_betas[idx], cum)
        masses[s, p] = cum[:, 0:1] + sps[idx][:, 0:1]
        emit(s, p, accs[s, p])

    def pending(s, mass):
        return jnp.where(win_start[s] > 0, mass, SB_DEAD_MASS)

    least = functools.reduce(jnp.minimum, [pending(s, m) for (s, _), m in masses.items()])
    older_end = win_start[n_sub - 1]

    @pl.when((older_end > 0) & (jnp.min(least) < SB_DEAD_MASS))
    def _():
        for idx, key in enumerate(accs):
            acc_ref[idx] = accs[key]
            run_ref[idx] = jnp.broadcast_to(masses[key], (2 * sub, LANES))

        def live(carry):
            end, least_mass = carry
            return (end > 0) & (least_mass < SB_DEAD_MASS)

        def older(carry):
            end, _ = carry
            start = pl.multiple_of(jnp.maximum(end - win, 0), sub)
            least_mass = None
            for idx, (s, p) in enumerate(accs):
                mask = col + start < jnp.minimum(win_start[s], end)
                run = run_ref[idx]
                pv, mass = key_tile(s, p, start, mask, jnp.concatenate([run] * (win // LANES), axis=1))
                acc_ref[idx] += pv
                run = run + jnp.broadcast_to(mass, run.shape)
                run_ref[idx] = run
                low = jnp.min(pending(s, run))
                least_mass = low if least_mass is None else jnp.minimum(least_mass, low)
            return start, least_mass

        lax.while_loop(live, older, (older_end, jnp.min(least)))
        for idx, (s, p) in enumerate(accs):
            emit(s, p, acc_ref[idx])


def _stick_break(u, suffix, width, q_col, k_col, v_col, g_col):
    b, s, _ = u.shape
    n_chain = (SB_TQ // SB_SUB) * (width // LANES)
    qb, kb, vb, gb = (c // width for c in (q_col, k_col, v_col, g_col))
    return pl.pallas_call(
        _stick_break_kernel,
        out_shape=jax.ShapeDtypeStruct((b, s, width), BF16),
        grid=(b, s // SB_TQ),
        in_specs=[
            pl.BlockSpec((None, SB_TQ, width), lambda bi, i: (bi, i, qb)),
            pl.BlockSpec((None, s, width), lambda bi, i: (bi, 0, kb)),
            pl.BlockSpec((None, s, width), lambda bi, i: (bi, 0, vb)),
            pl.BlockSpec((None, SB_TQ, width), lambda bi, i: (bi, i, gb)),
            pl.BlockSpec(suffix.shape, lambda bi, i: (0, 0)),
        ],
        out_specs=pl.BlockSpec((None, SB_TQ, width), lambda bi, i: (bi, i, 0)),
        scratch_shapes=[pltpu.VMEM((n_chain, 2 * SB_SUB, LANES), F32),
                        pltpu.VMEM((n_chain, 2 * SB_SUB, LANES), F32)],
        compiler_params=pltpu.CompilerParams(
            dimension_semantics=("parallel", "arbitrary"), vmem_limit_bytes=VMEM_LIMIT),
        name="stick_break",
    )(u, u, u, u, suffix)


def _mix_out_kernel(layer, seq_tiles, ua_ref, halo_ref, ysb_ref, m0_ref, m1_ref, m2_ref, x_ref,
                    pw32_ref, ps_ref, cw_ref, cb_ref, wb32_ref, wo32_ref, gp_ref, o_ref,
                    pw_ref, wb_ref, wo_ref):
    @pl.when(pl.program_id(0) == 0)
    def _():
        pw_ref[...] = pw32_ref[...].astype(BF16)
        wb_ref[...] = wb32_ref[...].astype(BF16)
        wo_ref[...] = wo32_ref[...].astype(BF16)

    tm = x_ref.shape[0]
    pool_scale, conv_bias, g_post = (r[layer:layer + 1, :] for r in (ps_ref, cb_ref, gp_ref))
    w = ysb_ref.shape[1]
    ti = pl.program_id(0) % seq_tiles

    def ext(col):
        head = jnp.where(ti == 0, 0.0, halo_ref[:, col:col + w].astype(F32))
        return jnp.concatenate([head, ua_ref[:, col:col + w].astype(F32)], axis=0)

    def back(a, r):
        return pltpu.roll(a, r, axis=0)

    pos = ti * tm + lax.broadcasted_iota(jnp.int32, (tm, 1), 0)
    v = ext(0)
    mixed = []
    for gi, win in enumerate(POOL_WINDOWS):
        vg = v[:, gi * POOL_GROUP:(gi + 1) * POOL_GROUP]
        ssum = vg
        span = 1
        while span < win:
            ssum = ssum + back(ssum, span)
            span *= 2
        inv_cnt = 1.0 / jnp.minimum(pos + 1, win).astype(F32)
        pooled = ssum[HALO:] * inv_cnt - vg[HALO:]
        mixed.append(jnp.dot(pooled.astype(BF16), pw_ref[gi], preferred_element_type=F32))
    y_pool = jnp.concatenate(mixed, axis=1) * pool_scale * _silu(ua_ref[:, w:2 * w].astype(F32))

    z = ext(4 * w) * ext(2 * w)
    y = cw_ref[0:1, :] * back(z, 2)[HALO:] + cw_ref[1:2, :] * back(z, 1)[HALO:] + cw_ref[2:3, :] * z[HALO:]
    y_conv = (ua_ref[:, 3 * w:4 * w].astype(F32) * (y + conv_bias)
              * _silu(ua_ref[:, 5 * w:6 * w].astype(F32)))

    merged = None
    for n, (yb, m_ref) in enumerate(((y_pool.astype(BF16), m0_ref), (y_conv.astype(BF16), m1_ref),
                                     (ysb_ref[...], m2_ref))):
        proj = jnp.dot(yb, wb_ref[n], preferred_element_type=F32)
        gated = _sigmoid(m_ref[...].astype(F32)) * proj
        merged = gated if merged is None else merged + gated

    out = jnp.dot(merged.astype(BF16), wo_ref[...], preferred_element_type=F32)
    ms = jnp.mean(out * out, axis=-1, keepdims=True)
    o_ref[...] = x_ref[...] + out * lax.rsqrt(ms + RMS_EPS) * g_post


def _mix_out(layer, u, ysb, x, seq_len, pool_w, pool_scale, conv_w, conv_b, w_branch, w_out, g_post):
    m, d = x.shape
    w = ysb.shape[1]
    tm = MIX_TM
    halo_blocks = tm // HALO
    merge_blk = (10 * w) // d

    def param(a):
        if a.ndim == 2:
            return pl.BlockSpec(a.shape, lambda i: (0, 0))
        return pl.BlockSpec((None,) + a.shape[1:], lambda i: (layer,) + (0,) * (a.ndim - 1))

    return pl.pallas_call(
        functools.partial(_mix_out_kernel, layer, seq_len // tm),
        out_shape=jax.ShapeDtypeStruct((m, d), F32),
        grid=(m // tm,),
        in_specs=[
            pl.BlockSpec((tm, 6 * w), lambda i: (i, 0)),
            pl.BlockSpec((HALO, 6 * w), lambda i: (jnp.maximum(i * halo_blocks - 1, 0), 0)),
            pl.BlockSpec((tm, w), lambda i: (i, 0)),
            pl.BlockSpec((tm, d), lambda i: (i, merge_blk)),
            pl.BlockSpec((tm, d), lambda i: (i, merge_blk + 1)),
            pl.BlockSpec((tm, d), lambda i: (i, merge_blk + 2)),
            pl.BlockSpec((tm, d), lambda i: (i, 0)),
            param(pool_w), param(pool_scale), param(conv_w), param(conv_b),
            param(w_branch), param(w_out), param(g_post),
        ],
        out_specs=pl.BlockSpec((tm, d), lambda i: (i, 0)),
        scratch_shapes=[pltpu.VMEM(a.shape[1:], BF16) for a in (pool_w, w_branch, w_out)],
        compiler_params=pltpu.CompilerParams(
            dimension_semantics=("arbitrary",), vmem_limit_bytes=VMEM_LIMIT),
        name="mix_out",
    )(u, u, ysb, u, u, u, x, pool_w, pool_scale, conv_w, conv_b, w_branch, w_out, g_post)


def kernel(x, pre_norm_g, w_in, pool_w, pool_scale, conv_w, conv_b, w_branch, w_out, post_norm_g):
    b, s, d = x.shape
    depth = w_in.shape[0]
    n_in = w_in.shape[2]
    w = pool_scale.shape[1]
    assert n_in == 10 * w + 3 * d and w % LANES == 0 and (10 * w) % d == 0
    assert (b * s) % IN_TM == 0 and n_in % IN_TN == 0 and s % SB_TQ == 0 and s % MIX_TM == 0
    assert s >= SB_WIN and SB_WIN % SB_SUB == 0
    key = jnp.arange(SB_WIN)
    suffix = (key[:, None] > key[None, :]).astype(BF16)
    xf = x.reshape(b * s, d)
    for l in range(depth):
        u = _in_proj(l, xf, pre_norm_g, w_in)
        ysb = _stick_break(u.reshape(b, s, n_in), suffix, w, 6 * w, 7 * w, 8 * w, 9 * w)
        xf = _mix_out(l, u, ysb.reshape(b * s, w), xf, s,
                      pool_w, pool_scale, conv_w, conv_b, w_branch, w_out, post_norm_g)
    return xf.reshape(b, s, d)
```

```python
import functools

import jax
import jax.numpy as jnp
from jax import lax
from jax.experimental import pallas as pl
from jax.experimental.pallas import tpu as pltpu

F32 = jnp.float32
BF16 = jnp.bfloat16

LANES = 128
HALO = 16
POOL_WINDOWS = (2, 4, 8, 16)
POOL_GROUP = 128
CONV_K = 3
HEAD_DIM = 64
RMS_EPS = 1e-6
LOG2E = 1.4426950408889634

IN_TM = 2048
IN_TN = 2048
SB_TQ = 1024
SB_SUB = 64
SB_WIN = 256
MIX_TM = 512
VMEM_LIMIT = 56 * 1024 * 1024
IN_VMEM_LIMIT = 60 * 1024 * 1024

SB_DEAD_MASS = 106.0


def _sigmoid(x):
    return 1.0 / (1.0 + jnp.exp2(x * -LOG2E))


def _silu(x):
    return x * _sigmoid(x)


def _softplus(z):
    return jnp.maximum(z, jnp.log(1.0 + jnp.exp(jnp.minimum(z, 44.0))))


def _in_proj_kernel(layer, x_ref, g_ref, w_ref, u_ref, h_ref):
    @pl.when(pl.program_id(1) == 0)
    def _():
        x = x_ref[...]
        ms = jnp.mean(x * x, axis=-1, keepdims=True)
        h_ref[...] = (x * lax.rsqrt(ms + RMS_EPS) * g_ref[layer:layer + 1, :]).astype(BF16)

    u_ref[...] = jnp.dot(h_ref[...], w_ref[...].astype(BF16),
                         preferred_element_type=F32).astype(u_ref.dtype)


def _in_proj(layer, x, g, w):
    m, d = x.shape
    n = w.shape[2]
    return pl.pallas_call(
        functools.partial(_in_proj_kernel, layer),
        out_shape=jax.ShapeDtypeStruct((m, n), BF16),
        grid=(m // IN_TM, n // IN_TN),
        in_specs=[
            pl.BlockSpec((IN_TM, d), lambda i, j: (i, 0)),
            pl.BlockSpec(g.shape, lambda i, j: (0, 0)),
            pl.BlockSpec((None, d, IN_TN), lambda i, j: (layer, 0, j)),
        ],
        out_specs=pl.BlockSpec((IN_TM, IN_TN), lambda i, j: (i, j)),
        scratch_shapes=[pltpu.VMEM((IN_TM, d), BF16)],
        compiler_params=pltpu.CompilerParams(
            dimension_semantics=("parallel", "arbitrary"), vmem_limit_bytes=IN_VMEM_LIMIT),
        name="in_proj",
    )(x, g, w)


def _proj_kernel(h_ref, w_ref, u_ref):
    u_ref[...] = jnp.dot(h_ref[...], w_ref[...].astype(BF16),
                         preferred_element_type=F32).astype(u_ref.dtype)


def _proj(layer, h, w):
    m, d = h.shape
    n = w.shape[2]
    return pl.pallas_call(
        _proj_kernel,
        out_shape=jax.ShapeDtypeStruct((m, n), BF16),
        grid=(m // IN_TM, n // IN_TN),
        in_specs=[
            pl.BlockSpec((IN_TM, d), lambda i, j: (i, 0)),
            pl.BlockSpec((None, d, IN_TN), lambda i, j: (layer, 0, j)),
        ],
        out_specs=pl.BlockSpec((IN_TM, IN_TN), lambda i, j: (i, j)),
        compiler_params=pltpu.CompilerParams(
            dimension_semantics=("parallel", "arbitrary"), vmem_limit_bytes=IN_VMEM_LIMIT),
        name="proj",
    )(h, w)


def _stick_break_kernel(q_ref, k_ref, v_ref, g_ref, suf_ref, o_ref, acc_ref, run_ref):
    sub, win = SB_SUB, SB_WIN
    n_sub = SB_TQ // sub
    n_pair = q_ref.shape[1] // LANES
    hist = win - sub
    q0 = pl.program_id(1) * SB_TQ

    lane = lax.broadcasted_iota(jnp.int32, (sub, LANES), 1)
    first = lane < HEAD_DIM
    col = lax.broadcasted_iota(jnp.int32, (2 * sub, win), 1)
    rsub = lax.broadcasted_iota(jnp.int32, (2 * sub, win), 0) & (sub - 1)
    scale = jnp.asarray(HEAD_DIM ** -0.5, BF16)
    nt = (((1,), (1,)), ((), ()))

    def stacked_q(s, p):
        q = q_ref[s * sub:(s + 1) * sub, p * LANES:(p + 1) * LANES]
        zero = jnp.zeros_like(q)
        return jnp.concatenate([jnp.where(first, q, zero), jnp.where(first, zero, q)], axis=0) * scale

    def logits(s, p, start):
        kt = k_ref[pl.ds(start, win), p * LANES:(p + 1) * LANES]
        return lax.dot_general(stacked_q(s, p), kt, nt, preferred_element_type=F32)

    def weigh(s, p, start, mask, log_beta, later):
        vt = v_ref[pl.ds(start, win), p * LANES:(p + 1) * LANES]
        w = jnp.where(mask, jnp.exp(log_beta - later), 0.0)
        return jnp.dot(w.astype(BF16), vt, preferred_element_type=F32)

    def key_tile(s, p, start, mask, run):
        z = logits(s, p, start)
        sp = _softplus(z)
        log_beta = z - sp
        sp = jnp.where(mask, sp, 0.0)
        cum = jnp.dot(sp.astype(BF16), suf_ref[...], preferred_element_type=F32)
        return weigh(s, p, start, mask, log_beta, cum + run), cum[:, 0:1] + sp[:, 0:1]

    def emit(s, p, acc):
        rows = slice(s * sub, (s + 1) * sub)
        cols = slice(p * LANES, (p + 1) * LANES)
        out = jnp.where(first, acc[:sub, :], acc[sub:, :])
        o_ref[rows, cols] = (out * _silu(g_ref[rows, cols].astype(F32))).astype(o_ref.dtype)

    win_start = [pl.multiple_of(jnp.maximum(q0 + s * sub - hist, 0), sub) for s in range(n_sub)]
    masks = [col < rsub + (q0 + s * sub - win_start[s]) for s in range(n_sub)]
    chains = [(s, p) for s in range(n_sub) for p in range(n_pair)]
    rows = 2 * sub
    zs = [logits(s, p, win_start[s]) for s, p in chains]
    sps, log_betas = [], []
    for (s, p), z in zip(chains, zs):
        sp = _softplus(z)
        log_betas.append(z - sp)
        sps.append(jnp.where(masks[s], sp, 0.0))
    cum_all = jnp.dot(jnp.concatenate([sp.astype(BF16) for sp in sps], axis=0), suf_ref[...],
                      preferred_element_type=F32)
    accs, masses = {}, {}
    for idx, (s, p) in enumerate(chains):
        cum = cum_all[idx * rows:(idx + 1) * rows]
        accs[s, p] = weigh(s, p, win_start[s], masks[s], log_betas[idx], cum)
        masses[s, p] = cum[:, 0:1] + sps[idx][:, 0:1]
        emit(s, p, accs[s, p])

    def pending(s, mass):
        return jnp.where(win_start[s] > 0, mass, SB_DEAD_MASS)

    least = functools.reduce(jnp.minimum, [pending(s, m) for (s, _), m in masses.items()])
    older_end = win_start[n_sub - 1]

    @pl.when((older_end > 0) & (jnp.min(least) < SB_DEAD_MASS))
    def _():
        for idx, key in enumerate(accs):
            acc_ref[idx] = accs[key]
            run_ref[idx] = jnp.broadcast_to(masses[key], (2 * sub, LANES))

        def live(carry):
            end, least_mass = carry
            return (end > 0) & (least_mass < SB_DEAD_MASS)

        def older(carry):
            end, _ = carry
            start = pl.multiple_of(jnp.maximum(end - win, 0), sub)
            least_mass = None
            for idx, (s, p) in enumerate(accs):
                mask = col + start < jnp.minimum(win_start[s], end)
                run = run_ref[idx]
                pv, mass = key_tile(s, p, start, mask, jnp.concatenate([run] * (win // LANES), axis=1))
                acc_ref[idx] += pv
                run = run + jnp.broadcast_to(mass, run.shape)
                run_ref[idx] = run
                low = jnp.min(pending(s, run))
                least_mass = low if least_mass is None else jnp.minimum(least_mass, low)
            return start, least_mass

        lax.while_loop(live, older, (older_end, jnp.min(least)))
        for idx, (s, p) in enumerate(accs):
            emit(s, p, acc_ref[idx])


def _stick_break(u, suffix, width, q_col, k_col, v_col, g_col):
    b, s, _ = u.shape
    n_chain = (SB_TQ // SB_SUB) * (width // LANES)
    qb, kb, vb, gb = (c // width for c in (q_col, k_col, v_col, g_col))
    return pl.pallas_call(
        _stick_break_kernel,
        out_shape=jax.ShapeDtypeStruct((b, s, width), BF16),
        grid=(b, s // SB_TQ),
        in_specs=[
            pl.BlockSpec((None, SB_TQ, width), lambda bi, i: (bi, i, qb)),
            pl.BlockSpec((None, s, width), lambda bi, i: (bi, 0, kb)),
            pl.BlockSpec((None, s, width), lambda bi, i: (bi, 0, vb)),
            pl.BlockSpec((None, SB_TQ, width), lambda bi, i: (bi, i, gb)),
            pl.BlockSpec(suffix.shape, lambda bi, i: (0, 0)),
        ],
        out_specs=pl.BlockSpec((None, SB_TQ, width), lambda bi, i: (bi, i, 0)),
        scratch_shapes=[pltpu.VMEM((n_chain, 2 * SB_SUB, LANES), F32),
                        pltpu.VMEM((n_chain, 2 * SB_SUB, LANES), F32)],
        compiler_params=pltpu.CompilerParams(
            dimension_semantics=("parallel", "arbitrary"), vmem_limit_bytes=VMEM_LIMIT),
        name="stick_break",
    )(u, u, u, u, suffix)


def _mix_out_kernel(layer, seq_tiles, emit_next, ua_ref, halo_ref, ysb_ref, m0_ref, m1_ref, m2_ref,
                    x_ref, pw32_ref, ps_ref, cw_ref, cb_ref, wb32_ref, wo32_ref, gp_ref, gn_ref,
                    o_ref, *rest):
    hn_ref = rest[0] if emit_next else None
    pw_ref, wb_ref, wo_ref = rest[-3:]
    @pl.when(pl.program_id(0) == 0)
    def _():
        pw_ref[...] = pw32_ref[...].astype(BF16)
        wb_ref[...] = wb32_ref[...].astype(BF16)
        wo_ref[...] = wo32_ref[...].astype(BF16)

    tm = x_ref.shape[0]
    pool_scale, conv_bias, g_post = (r[layer:layer + 1, :] for r in (ps_ref, cb_ref, gp_ref))
    w = ysb_ref.shape[1]
    ti = pl.program_id(0) % seq_tiles

    def ext(col):
        head = jnp.where(ti == 0, 0.0, halo_ref[:, col:col + w].astype(F32))
        return jnp.concatenate([head, ua_ref[:, col:col + w].astype(F32)], axis=0)

    def back(a, r):
        return pltpu.roll(a, r, axis=0)

    pos = ti * tm + lax.broadcasted_iota(jnp.int32, (tm, 1), 0)
    v = ext(0)
    mixed = []
    for gi, win in enumerate(POOL_WINDOWS):
        vg = v[:, gi * POOL_GROUP:(gi + 1) * POOL_GROUP]
        ssum = vg
        span = 1
        while span < win:
            ssum = ssum + back(ssum, span)
            span *= 2
        inv_cnt = 1.0 / jnp.minimum(pos + 1, win).astype(F32)
        pooled = ssum[HALO:] * inv_cnt - vg[HALO:]
        mixed.append(jnp.dot(pooled.astype(BF16), pw_ref[gi], preferred_element_type=F32))
    y_pool = jnp.concatenate(mixed, axis=1) * pool_scale * _silu(ua_ref[:, w:2 * w].astype(F32))

    z = ext(4 * w) * ext(2 * w)
    y = cw_ref[0:1, :] * back(z, 2)[HALO:] + cw_ref[1:2, :] * back(z, 1)[HALO:] + cw_ref[2:3, :] * z[HALO:]
    y_conv = (ua_ref[:, 3 * w:4 * w].astype(F32) * (y + conv_bias)
              * _silu(ua_ref[:, 5 * w:6 * w].astype(F32)))

    merged = None
    for n, (yb, m_ref) in enumerate(((y_pool.astype(BF16), m0_ref), (y_conv.astype(BF16), m1_ref),
                                     (ysb_ref[...], m2_ref))):
        proj = jnp.dot(yb, wb_ref[n], preferred_element_type=F32)
        gated = _sigmoid(m_ref[...].astype(F32)) * proj
        merged = gated if merged is None else merged + gated

    out = jnp.dot(merged.astype(BF16), wo_ref[...], preferred_element_type=F32)
    ms = jnp.mean(out * out, axis=-1, keepdims=True)
    x_new = x_ref[...] + out * lax.rsqrt(ms + RMS_EPS) * g_post
    o_ref[...] = x_new
    if emit_next:
        ms_next = jnp.mean(x_new * x_new, axis=-1, keepdims=True)
        g_next = gn_ref[layer + 1:layer + 2, :]
        hn_ref[...] = (x_new * lax.rsqrt(ms_next + RMS_EPS) * g_next).astype(BF16)


def _mix_out(layer, emit_next, u, ysb, x, seq_len, pool_w, pool_scale, conv_w, conv_b, w_branch, w_out,
             g_post, g_pre):
    m, d = x.shape
    w = ysb.shape[1]
    tm = MIX_TM
    halo_blocks = tm // HALO
    merge_blk = (10 * w) // d

    def param(a):
        if a.ndim == 2:
            return pl.BlockSpec(a.shape, lambda i: (0, 0))
        return pl.BlockSpec((None,) + a.shape[1:], lambda i: (layer,) + (0,) * (a.ndim - 1))

    row_tile = pl.BlockSpec((tm, d), lambda i: (i, 0))
    out_shape = [jax.ShapeDtypeStruct((m, d), F32)] + [jax.ShapeDtypeStruct((m, d), BF16)] * emit_next
    return pl.pallas_call(
        functools.partial(_mix_out_kernel, layer, seq_len // tm, emit_next),
        out_shape=out_shape,
        grid=(m // tm,),
        in_specs=[
            pl.BlockSpec((tm, 6 * w), lambda i: (i, 0)),
            pl.BlockSpec((HALO, 6 * w), lambda i: (jnp.maximum(i * halo_blocks - 1, 0), 0)),
            pl.BlockSpec((tm, w), lambda i: (i, 0)),
            pl.BlockSpec((tm, d), lambda i: (i, merge_blk)),
            pl.BlockSpec((tm, d), lambda i: (i, merge_blk + 1)),
            pl.BlockSpec((tm, d), lambda i: (i, merge_blk + 2)),
            pl.BlockSpec((tm, d), lambda i: (i, 0)),
            param(pool_w), param(pool_scale), param(conv_w), param(conv_b),
            param(w_branch), param(w_out), param(g_post), param(g_pre),
        ],
        out_specs=[row_tile] * len(out_shape),
        scratch_shapes=[pltpu.VMEM(a.shape[1:], BF16) for a in (pool_w, w_branch, w_out)],
        compiler_params=pltpu.CompilerParams(
            dimension_semantics=("arbitrary",), vmem_limit_bytes=VMEM_LIMIT),
        name="mix_out",
    )(u, u, ysb, u, u, u, x, pool_w, pool_scale, conv_w, conv_b, w_branch, w_out, g_post, g_pre)


def kernel(x, pre_norm_g, w_in, pool_w, pool_scale, conv_w, conv_b, w_branch, w_out, post_norm_g):
    b, s, d = x.shape
    depth = w_in.shape[0]
    n_in = w_in.shape[2]
    w = pool_scale.shape[1]
    assert n_in == 10 * w + 3 * d and w % LANES == 0 and (10 * w) % d == 0
    assert (b * s) % IN_TM == 0 and n_in % IN_TN == 0 and s % SB_TQ == 0 and s % MIX_TM == 0
    assert s >= SB_WIN and SB_WIN % SB_SUB == 0
    key = jnp.arange(SB_WIN)
    suffix = (key[:, None] > key[None, :]).astype(BF16)
    xf = x.reshape(b * s, d)
    h = None
    for l in range(depth):
        u = _in_proj(l, xf, pre_norm_g, w_in) if h is None else _proj(l, h, w_in)
        ysb = _stick_break(u.reshape(b, s, n_in), suffix, w, 6 * w, 7 * w, 8 * w, 9 * w)
        outs = _mix_out(l, l + 1 < depth, u, ysb.reshape(b * s, w), xf, s,
                        pool_w, pool_scale, conv_w, conv_b, w_branch, w_out, post_norm_g, pre_norm_g)
        xf, h = outs[0], (outs[1] if l + 1 < depth else None)
    return xf.reshape(b, s, d)
```
